```python
import jax, jax.numpy as jnp
from jax import lax
import numpy as np

D_MODEL = 1024
BATCH = 4
SEQ = 4096
DEPTH = 1

HEAD_DIM = 64
SWA_Q_HEADS = 8
SWA_KV_HEADS = 2
SWA_GROUP = SWA_Q_HEADS // SWA_KV_HEADS
WINDOW = 128
FOX_HEADS = 8
Q_BLOCK = 128
N_EXPERTS = 32
TOP_K = 4
D_FF = D_MODEL
SWIGLU_LIMIT = 7.0
SWIGLU_ALPHA = 1.702
EXPERT_BLOCK = 256
LN_EPS = 1e-5
DEEPNORM_ALPHA = (2.0 * DEPTH) ** 0.25
DEEPNORM_BETA = (8.0 * DEPTH) ** -0.25

SWA_Q_W = SWA_Q_HEADS * HEAD_DIM
SWA_KV_W = SWA_KV_HEADS * HEAD_DIM
FOX_W = FOX_HEADS * HEAD_DIM
IN_SPLITS = (SWA_Q_W, SWA_KV_W, SWA_KV_W, FOX_W, FOX_W, FOX_W, FOX_HEADS, D_MODEL, D_MODEL)
IN_WIDTH = sum(IN_SPLITS)

kernel_name = 'hybrid_swa_sink_fox_moe_deepnorm'


def _alibi_slopes(n):
    return jnp.asarray(2.0 ** (-8.0 * np.arange(1, n + 1) / n), dtype=jnp.float32)


def _layer_norm(x, g, b):
    xf = x.astype(jnp.float32)
    mu = jnp.mean(xf, axis=-1, keepdims=True)
    var = jnp.mean(jnp.square(xf - mu), axis=-1, keepdims=True)
    y = (xf - mu) * lax.rsqrt(var + LN_EPS) * g.astype(jnp.float32) + b.astype(jnp.float32)
    return y.astype(x.dtype)


def _sliding_window_sink_attention(q, k, v, sink):
    B, S = q.shape[0], q.shape[1]
    nb = S // WINDOW
    qb = q.reshape(B, nb, WINDOW, SWA_KV_HEADS, SWA_GROUP, HEAD_DIM)

    def with_prev(t):
        tb = t.reshape(B, nb, WINDOW, SWA_KV_HEADS, HEAD_DIM)
        prev = jnp.pad(tb[:, :-1], ((0, 0), (1, 0), (0, 0), (0, 0), (0, 0)))
        return jnp.concatenate([prev, tb], axis=2)

    kx, vx = with_prev(k), with_prev(v)
    scores = jnp.einsum('bnqkgd,bnskd->bnkgqs', qb, kx).astype(jnp.float32) * (HEAD_DIM ** -0.5)
    qi = jnp.arange(WINDOW)[:, None]
    sj = jnp.arange(2 * WINDOW)[None, :]
    dist = qi + WINDOW - sj
    key_pos = jnp.arange(nb)[:, None, None] * WINDOW - WINDOW + sj[None]
    valid = (dist >= 0) & (dist < WINDOW) & (key_pos >= 0)
    slopes = _alibi_slopes(SWA_Q_HEADS).reshape(SWA_KV_HEADS, SWA_GROUP)
    scores = scores - slopes[:, :, None, None] * dist.astype(jnp.float32)
    scores = jnp.where(valid[None, :, None, None], scores, -jnp.inf)
    sink_col = jnp.broadcast_to(sink.astype(jnp.float32)[None, None, :, :, None, None], scores.shape[:-1] + (1,))
    probs = jax.nn.softmax(jnp.concatenate([scores, sink_col], axis=-1), axis=-1)[..., :-1]
    out = jnp.einsum('bnkgqs,bnskd->bnqkgd', probs.astype(v.dtype), vx)
    return out.reshape(B, S, SWA_Q_W)


def _forgetting_attention(q, k, v, log_f):
    B, S = q.shape[0], q.shape[1]
    nb = S // Q_BLOCK
    c = jnp.cumsum(log_f, axis=1)
    c_keys = jnp.transpose(c, (0, 2, 1))
    key_pos = jnp.arange(S)
    qs = jnp.swapaxes(q.reshape(B, nb, Q_BLOCK, FOX_HEADS, HEAD_DIM), 0, 1)
    cs = jnp.swapaxes(c.reshape(B, nb, Q_BLOCK, FOX_HEADS), 0, 1)

    def block(args):
        qb, cb, i = args
        s = jnp.einsum('bqhd,bshd->bhqs', qb, k).astype(jnp.float32) * (HEAD_DIM ** -0.5)
        s = s + jnp.transpose(cb, (0, 2, 1))[..., None] - c_keys[:, :, None, :]
        q_pos = i * Q_BLOCK + jnp.arange(Q_BLOCK)
        s = jnp.where(key_pos[None, :] <= q_pos[:, None], s, -jnp.inf)
        p = jax.nn.softmax(s, axis=-1)
        return jnp.einsum('bhqs,bshd->bqhd', p.astype(v.dtype), v)

    out = lax.map(block, (qs, cs, jnp.arange(nb, dtype=jnp.int32)))
    return jnp.swapaxes(out, 0, 1).reshape(B, S, FOX_W)


def _moe(x, w_router, b_router, w_gate_up, b_gate_up, w_down, b_down):
    B, S, D = x.shape
    N = B * S
    NK = N * TOP_K
    xt = x.reshape(N, D)
    logits = (xt @ w_router + b_router).astype(jnp.float32)
    top_val, top_idx = lax.top_k(logits, TOP_K)
    gates = jax.nn.softmax(top_val, axis=-1)
    flat_e = top_idx.reshape(-1)
    flat_tok = jnp.arange(NK, dtype=jnp.int32) // TOP_K
    flat_w = gates.reshape(-1)
    order = jnp.argsort(flat_e)
    sorted_e = flat_e[order]
    counts = jnp.bincount(flat_e, length=N_EXPERTS)
    padded = ((counts + EXPERT_BLOCK - 1) // EXPERT_BLOCK) * EXPERT_BLOCK
    ends = jnp.cumsum(padded)
    starts = ends - padded
    count_starts = jnp.cumsum(counts) - counts
    rank = jnp.arange(NK, dtype=jnp.int32) - count_starts[sorted_e]
    slot = starts[sorted_e] + rank
    R = ((NK + EXPERT_BLOCK - 1) // EXPERT_BLOCK) * EXPERT_BLOCK + N_EXPERTS * EXPERT_BLOCK
    n_blocks = R // EXPERT_BLOCK
    slot_tok = jnp.zeros((R,), jnp.int32).at[slot].set(flat_tok[order])
    slot_w = jnp.zeros((R,), jnp.float32).at[slot].set(flat_w[order])
    block_e = jnp.minimum(jnp.searchsorted(ends, jnp.arange(n_blocks) * EXPERT_BLOCK, side='right'), N_EXPERTS - 1)
    xs = xt[slot_tok].reshape(n_blocks, EXPERT_BLOCK, D)

    def expert_block(args):
        xb, e = args
        h = xb @ w_gate_up[e] + b_gate_up[e]
        gate, up = h[:, :D_FF], h[:, D_FF:]
        gate = jnp.minimum(gate, SWIGLU_LIMIT)
        up = jnp.clip(up, -SWIGLU_LIMIT, SWIGLU_LIMIT)
        act = gate * jax.nn.sigmoid(gate * SWIGLU_ALPHA) * (up + 1.0)
        return act @ w_down[e] + b_down[e]

    ys = lax.map(expert_block, (xs, block_e)).reshape(R, D)
    out = jnp.zeros((N, D), ys.dtype).at[slot_tok].add(ys * slot_w[:, None].astype(ys.dtype))
    return out.reshape(B, S, D)


def setup_inputs(seed: int = 0) -> dict:
    key = jax.random.key(seed)
    ks = jax.random.split(key, 20)
    beta = DEEPNORM_BETA

    def dense(k, shape, fan_in, scale=1.0):
        return jax.random.normal(k, shape, jnp.float32) * (scale * fan_in ** -0.5)

    kin = jax.random.split(ks[0], len(IN_SPLITS))
    in_scales = (1.0, 1.0, beta, 1.0, 1.0, beta, 1.0, 1.0, 1.0)
    w_in = jnp.concatenate([dense(kin[i], (DEPTH, D_MODEL, w), D_MODEL, in_scales[i]) for i, w in enumerate(IN_SPLITS)], axis=-1)
    return {
        'x': jax.random.normal(ks[1], (BATCH, SEQ, D_MODEL), jnp.float32),
        'w_in': w_in,
        'b_forget': 2.0 + 4.0 * jax.random.uniform(ks[2], (DEPTH, FOX_HEADS), jnp.float32),
        'sink': 0.5 * jax.random.normal(ks[3], (DEPTH, SWA_KV_HEADS, SWA_GROUP), jnp.float32),
        'w_proj_swa': dense(ks[4], (DEPTH, SWA_Q_W, D_MODEL), SWA_Q_W, beta),
        'w_proj_fox': dense(ks[5], (DEPTH, FOX_W, D_MODEL), FOX_W, beta),
        'w_out': dense(ks[6], (DEPTH, D_MODEL, D_MODEL), D_MODEL, beta),
        'ln1_g': 1.0 + 0.02 * jax.random.normal(ks[7], (DEPTH, D_MODEL), jnp.float32),
        'ln1_b': 0.02 * jax.random.normal(ks[8], (DEPTH, D_MODEL), jnp.float32),
        'w_router': dense(ks[9], (DEPTH, D_MODEL, N_EXPERTS), D_MODEL),
        'b_router': 0.01 * jax.random.normal(ks[10], (DEPTH, N_EXPERTS), jnp.float32),
        'w_gate_up': dense(ks[11], (DEPTH, N_EXPERTS, D_MODEL, 2 * D_FF), D_MODEL, beta),
        'b_gate_up': 0.01 * jax.random.normal(ks[12], (DEPTH, N_EXPERTS, 2 * D_FF), jnp.float32),
        'w_down': dense(ks[13], (DEPTH, N_EXPERTS, D_FF, D_MODEL), D_FF, beta),
        'b_down': 0.01 * jax.random.normal(ks[14], (DEPTH, N_EXPERTS, D_MODEL), jnp.float32),
        'ln2_g': 1.0 + 0.02 * jax.random.normal(ks[15], (DEPTH, D_MODEL), jnp.float32),
        'ln2_b': 0.02 * jax.random.normal(ks[16], (DEPTH, D_MODEL), jnp.float32),
    }


def reference(x, w_in, b_forget, sink, w_proj_swa, w_proj_fox, w_out, ln1_g, ln1_b, w_router, b_router, w_gate_up, b_gate_up, w_down, b_down, ln2_g, ln2_b):
    B, S, _ = x.shape
    split_points = [int(p) for p in np.cumsum(IN_SPLITS)[:-1]]
    h = x
    for layer in range(DEPTH):
        proj = h @ w_in[layer]
        qa, ka, va, qf, kf, vf, fl, ga, gf = jnp.split(proj, split_points, axis=-1)
        ya = _sliding_window_sink_attention(
            qa.reshape(B, S, SWA_KV_HEADS, SWA_GROUP, HEAD_DIM),
            ka.reshape(B, S, SWA_KV_HEADS, HEAD_DIM),
            va.reshape(B, S, SWA_KV_HEADS, HEAD_DIM),
            sink[layer]) @ w_proj_swa[layer]
        log_f = jax.nn.log_sigmoid(fl.astype(jnp.float32) + b_forget[layer].astype(jnp.float32))
        yf = _forgetting_attention(
            qf.reshape(B, S, FOX_HEADS, HEAD_DIM),
            kf.reshape(B, S, FOX_HEADS, HEAD_DIM),
            vf.reshape(B, S, FOX_HEADS, HEAD_DIM),
            log_f) @ w_proj_fox[layer]
        mix = jax.nn.sigmoid(ga) * ya + jax.nn.sigmoid(gf) * yf
        h = _layer_norm(DEEPNORM_ALPHA * h + mix @ w_out[layer], ln1_g[layer], ln1_b[layer])
        m = _moe(h, w_router[layer], b_router[layer], w_gate_up[layer], b_gate_up[layer], w_down[layer], b_down[layer])
        h = _layer_norm(DEEPNORM_ALPHA * h + m, ln2_g[layer], ln2_b[layer])
    return h
```

```python
import functools

import numpy as np
import jax
import jax.numpy as jnp
from jax import lax
from jax.experimental import pallas as pl
from jax.experimental.pallas import tpu as pltpu

HEAD_DIM = 64
SWA_Q_HEADS = 8
SWA_KV_HEADS = 2
SWA_GROUP = SWA_Q_HEADS // SWA_KV_HEADS
WINDOW = 128
FOX_HEADS = 8
N_EXPERTS = 32
TOP_K = 4
SWIGLU_LIMIT = 7.0
SWIGLU_ALPHA = 1.702
LN_EPS = 1e-5

LANES = 128
SEQ_TILE = 512
EXPERT_ROWS = 256
VMEM_LIMIT = 56 * 1024 * 1024

SWA_Q_W = SWA_Q_HEADS * HEAD_DIM
SWA_KV_W = SWA_KV_HEADS * HEAD_DIM
FOX_W = FOX_HEADS * HEAD_DIM
ATT_W = SWA_Q_W + 2 * SWA_KV_W + 3 * FOX_W
NEG_INF = float("-inf")

_F32 = jnp.float32
_BF16 = jnp.bfloat16


def _dot(a, b):
    return jnp.dot(a, b, preferred_element_type=_F32)


def _dot_nt(a, b):
    return lax.dot_general(a, b, (((1,), (1,)), ((), ())), preferred_element_type=_F32)


def _params(*sem):
    return pltpu.CompilerParams(dimension_semantics=sem, vmem_limit_bytes=VMEM_LIMIT)


def _layer_norm(z, g, b):
    mu = jnp.mean(z, axis=-1, keepdims=True)
    zc = z - mu
    var = jnp.mean(zc * zc, axis=-1, keepdims=True)
    return zc * lax.rsqrt(var + LN_EPS) * g + b


def _in_proj_kernel(x_ref, w_ref, bf_ref, tri_ref, qkv_ref, ccol_ref, crow_ref, carry_ref):
    @pl.when(pl.program_id(1) == 0)
    def _():
        carry_ref[...] = jnp.zeros_like(carry_ref)

    xb = x_ref[0].astype(_BF16)
    acc = _dot(xb, w_ref[...])
    qkv_ref[0] = acc[:, :ATT_W].astype(_BF16)
    z = acc[:, ATT_W:] + bf_ref[...]
    log_f = jnp.minimum(z, 0.0) - jnp.log1p(jnp.exp(-jnp.abs(z)))
    tri = tri_ref[...]
    p1 = log_f.astype(_BF16)
    r1 = log_f - p1.astype(_F32)
    p2 = r1.astype(_BF16)
    p3 = (r1 - p2.astype(_F32)).astype(_BF16)
    c = _dot(tri, p1) + _dot(tri, p2) + _dot(tri, p3) + carry_ref[0:1, :]
    t = c.shape[0]
    carry_ref[...] = jnp.broadcast_to(c[t - 1:t, :], carry_ref.shape)
    ccol_ref[0] = c
    ct = c.T
    for p in range(FOX_HEADS // 2):
        crow_ref[0, p, 0] = ct[2 * p:2 * p + 2, :]


def _in_proj(x, w_att, bf_pad, tri):
    b, s, d = x.shape
    t = SEQ_TILE
    wn = w_att.shape[1]
    return pl.pallas_call(
        _in_proj_kernel,
        grid=(b, s // t),
        in_specs=[
            pl.BlockSpec((1, t, d), lambda i, j: (i, j, 0)),
            pl.BlockSpec((d, wn), lambda i, j: (0, 0)),
            pl.BlockSpec((1, LANES), lambda i, j: (0, 0)),
            pl.BlockSpec((t, t), lambda i, j: (0, 0)),
        ],
        out_specs=[
            pl.BlockSpec((1, t, ATT_W), lambda i, j: (i, j, 0)),
            pl.BlockSpec((1, t, LANES), lambda i, j: (i, j, 0)),
            pl.BlockSpec((1, FOX_HEADS // 2, 1, 2, t), lambda i, j: (i, 0, j, 0, 0)),
        ],
        out_shape=[
            jax.ShapeDtypeStruct((b, s, ATT_W), _BF16),
            jax.ShapeDtypeStruct((b, s, LANES), _F32),
            jax.ShapeDtypeStruct((b, FOX_HEADS // 2, s // t, 2, t), _F32),
        ],
        scratch_shapes=[pltpu.VMEM((8, LANES), _F32)],
        compiler_params=_params("arbitrary", "arbitrary"),
        name="in_proj",
    )(x, w_att, bf_pad, tri)


def _alibi_slopes(n):
    return [float(v) for v in np.asarray(2.0 ** (-8.0 * np.arange(1, n + 1) / n), dtype=np.float32)]


def _swa_kernel(sink_ref, q_ref, kp_ref, kc_ref, vp_ref, vc_ref, o_ref):
    blk = pl.program_id(1)
    w = WINDOW
    q = q_ref[0] * jnp.asarray(HEAD_DIM ** -0.5, _BF16)
    k = jnp.concatenate([kp_ref[0], kc_ref[0]], axis=0)
    v = jnp.concatenate([vp_ref[0], vc_ref[0]], axis=0)
    row = lax.broadcasted_iota(jnp.int32, (w, 2 * w), 0)
    col = lax.broadcasted_iota(jnp.int32, (w, 2 * w), 1)
    dist = row + w - col
    valid = (dist >= 0) & (dist < w) & ((col >= w) | (blk > 0))
    distf = dist.astype(_F32)
    slopes = _alibi_slopes(SWA_Q_HEADS)
    outs = []
    for h in range(SWA_Q_HEADS):
        g = h // SWA_GROUP
        qh = q[:, h * HEAD_DIM:(h + 1) * HEAD_DIM]
        kg = k[:, g * HEAD_DIM:(g + 1) * HEAD_DIM]
        vg = v[:, g * HEAD_DIM:(g + 1) * HEAD_DIM]
        sc = _dot_nt(qh, kg) - slopes[h] * distf
        sc = jnp.where(valid, sc, NEG_INF)
        sk = sink_ref[h]
        m = jnp.maximum(jnp.max(sc, axis=1, keepdims=True), sk)
        p = jnp.exp(sc - m)
        denom = jnp.sum(p, axis=1, keepdims=True) + jnp.exp(sk - m)
        o = _dot(p.astype(_BF16), vg)
        outs.append(o / denom)
    o_ref[0] = jnp.concatenate(outs, axis=1).astype(_BF16)


def _swa(qkv, sink_flat):
    b, s, _ = qkv.shape
    w = WINDOW
    kcol = SWA_Q_W // LANES
    vcol = kcol + SWA_KV_W // LANES
    grid_spec = pltpu.PrefetchScalarGridSpec(
        num_scalar_prefetch=1,
        grid=(b, s // w),
        in_specs=[
            pl.BlockSpec((1, w, SWA_Q_W), lambda i, j, sk: (i, j, 0)),
            pl.BlockSpec((1, w, SWA_KV_W), lambda i, j, sk: (i, jnp.maximum(j - 1, 0), kcol)),
            pl.BlockSpec((1, w, SWA_KV_W), lambda i, j, sk: (i, j, kcol)),
            pl.BlockSpec((1, w, SWA_KV_W), lambda i, j, sk: (i, jnp.maximum(j - 1, 0), vcol)),
            pl.BlockSpec((1, w, SWA_KV_W), lambda i, j, sk: (i, j, vcol)),
        ],
        out_specs=pl.BlockSpec((1, w, SWA_Q_W), lambda i, j, sk: (i, j, 0)),
    )
    return pl.pallas_call(
        _swa_kernel,
        grid_spec=grid_spec,
        out_shape=jax.ShapeDtypeStruct((b, s, SWA_Q_W), _BF16),
        compiler_params=_params("arbitrary", "arbitrary"),
        name="swa",
    )(sink_flat, qkv, qkv, qkv, qkv, qkv)


def _fox_kernel(q_ref, k_ref, v_ref, ccol_ref, crow_ref, o_ref, m_ref, l_ref, acc_ref):
    pair = pl.program_id(1)
    qi = pl.program_id(2)
    t = q_ref.shape[1]
    lane = lax.broadcasted_iota(jnp.int32, (t, LANES), 1)
    q = q_ref[0] * jnp.asarray(HEAD_DIM ** -0.5, _BF16)
    zero = jnp.zeros_like(q)
    qh = [jnp.where(lane < HEAD_DIM, q, zero), jnp.where(lane >= HEAD_DIM, q, zero)]
    ccol = ccol_ref[0]
    cq = [jnp.sum(jnp.where(lane == 2 * pair + hh, ccol, 0.0), axis=1, keepdims=True) for hh in range(2)]

    m_ref[...] = jnp.full_like(m_ref, NEG_INF)
    l_ref[...] = jnp.zeros_like(l_ref)
    acc_ref[...] = jnp.zeros_like(acc_ref)

    def step(j, masked):
        start = pl.multiple_of(j * t, t)
        k = k_ref[0, pl.ds(start, t), :]
        v = v_ref[0, pl.ds(start, t), :]
        ck = crow_ref[0, 0, j]
        if masked:
            r = lax.broadcasted_iota(jnp.int32, (t, t), 0)
            c = lax.broadcasted_iota(jnp.int32, (t, t), 1)
            keep = c <= r
        for hh in range(2):
            sc = _dot_nt(qh[hh], k) - ck[hh:hh + 1, :]
            if masked:
                sc = jnp.where(keep, sc, NEG_INF)
            m_old = m_ref[hh]
            m_new = jnp.maximum(m_old, jnp.max(sc, axis=1, keepdims=True) + cq[hh])
            alpha = jnp.exp(m_old - m_new)
            p = jnp.exp(sc - (m_new - cq[hh]))
            l_ref[hh] = alpha * l_ref[hh] + jnp.sum(p, axis=1, keepdims=True)
            acc_ref[hh] = alpha * acc_ref[hh] + _dot(p.astype(_BF16), v)
            m_ref[hh] = m_new

    def body(j, carry):
        step(j, False)
        return carry

    lax.fori_loop(0, qi, body, 0)
    step(qi, True)
    o0 = acc_ref[0] / l_ref[0]
    o1 = acc_ref[1] / l_ref[1]
    o_ref[0] = jnp.where(lane < HEAD_DIM, o0, o1).astype(_BF16)


def _fox(qkv, ccol, crow):
    b, s, _ = qkv.shape
    t = SEQ_TILE
    pairs = FOX_HEADS // 2
    qcol = (SWA_Q_W + 2 * SWA_KV_W) // LANES
    kcol = qcol + FOX_W // LANES
    vcol = kcol + FOX_W // LANES
    return pl.pallas_call(
        _fox_kernel,
        grid=(b, pairs, s // t),
        in_specs=[
            pl.BlockSpec((1, t, LANES), lambda i, p, j: (i, j, qcol + p)),
            pl.BlockSpec((1, s, LANES), lambda i, p, j: (i, 0, kcol + p)),
            pl.BlockSpec((1, s, LANES), lambda i, p, j: (i, 0, vcol + p)),
            pl.BlockSpec((1, t, LANES), lambda i, p, j: (i, j, 0)),
            pl.BlockSpec((1, 1, s // t, 2, t), lambda i, p, j: (i, p, 0, 0, 0)),
        ],
        out_specs=pl.BlockSpec((1, t, LANES), lambda i, p, j: (i, j, p)),
        out_shape=jax.ShapeDtypeStruct((b, s, FOX_W), _BF16),
        scratch_shapes=[
            pltpu.VMEM((2, t, 1), _F32),
            pltpu.VMEM((2, t, 1), _F32),
            pltpu.VMEM((2, t, LANES), _F32),
        ],
        compiler_params=_params("arbitrary", "arbitrary", "arbitrary"),
        name="fox",
    )(qkv, qkv, qkv, ccol, crow)


def _mix_kernel(alpha, x_ref, aa_ref, af_ref, wg_ref, wps_ref, wpf_ref, wo_ref, g1_ref, b1_ref,
                wr_ref, br_ref, h_ref, idx_ref, gate_ref):
    d = x_ref.shape[1]
    t = x_ref.shape[0]
    x = x_ref[...]
    gates = _dot(x.astype(_BF16), wg_ref[...])
    ya = _dot(aa_ref[...], wps_ref[...])
    yf = _dot(af_ref[...], wpf_ref[...])
    mix = jax.nn.sigmoid(gates[:, :d]) * ya + jax.nn.sigmoid(gates[:, d:]) * yf
    z = alpha * x + _dot(mix.astype(_BF16), wo_ref[...])
    h = _layer_norm(z, g1_ref[...], b1_ref[...])
    h_ref[...] = h

    logits = _dot(h.astype(_BF16), wr_ref[...]) + br_ref[...]
    lt = logits.T[:N_EXPERTS, :]
    eidx = lax.broadcasted_iota(jnp.int32, (N_EXPERTS, t), 0)
    work = lt
    vals, idxs = [], []
    for _ in range(TOP_K):
        mk = jnp.max(work, axis=0, keepdims=True)
        ik = jnp.min(jnp.where(work == mk, eidx, N_EXPERTS), axis=0, keepdims=True)
        vals.append(mk)
        idxs.append(ik)
        work = jnp.where(eidx == ik, NEG_INF, work)
    ex = [jnp.exp(vk - vals[0]) for vk in vals]
    denom = ex[0] + ex[1] + ex[2] + ex[3]
    idx_ref[...] = jnp.concatenate(idxs + [jnp.zeros((8 - TOP_K, t), jnp.int32)], axis=0)
    gt = jnp.concatenate([e / denom for e in ex] + [jnp.zeros((LANES - TOP_K, t), _F32)], axis=0)
    gate_ref[...] = gt.T


def _mix(alpha, x2, att_a, att_f, wg, wps, wpf, wo, g1, b1, wr, br):
    n, d = x2.shape
    t = SEQ_TILE
    const = lambda shape: pl.BlockSpec(shape, lambda i: (0, 0))
    return pl.pallas_call(
        functools.partial(_mix_kernel, alpha),
        grid=(n // t,),
        in_specs=[
            pl.BlockSpec((t, d), lambda i: (i, 0)),
            pl.BlockSpec((t, SWA_Q_W), lambda i: (i, 0)),
            pl.BlockSpec((t, FOX_W), lambda i: (i, 0)),
            const(wg.shape), const(wps.shape), const(wpf.shape), const(wo.shape),
            const(g1.shape), const(b1.shape), const(wr.shape), const(br.shape),
        ],
        out_specs=[
            pl.BlockSpec((t, d), lambda i: (i, 0)),
            pl.BlockSpec((8, t), lambda i: (0, i)),
            pl.BlockSpec((t, LANES), lambda i: (i, 0)),
        ],
        out_shape=[
            jax.ShapeDtypeStruct((n, d), _F32),
            jax.ShapeDtypeStruct((8, n), jnp.int32),
            jax.ShapeDtypeStruct((n, LANES), _F32),
        ],
        compiler_params=_params("arbitrary"),
        name="mix",
    )(x2, att_a, att_f, wg, wps, wpf, wo, g1, b1, wr, br)


def _route_kernel(idx_ref, tri_ref, slot_ref, meta_ref, cnt_ref, start_ref, run_ref):
    phase = pl.program_id(0)
    i = pl.program_id(1)
    t = idx_ref.shape[1]
    blk = EXPERT_ROWS
    eidx = lax.broadcasted_iota(jnp.int32, (N_EXPERTS, t), 0)
    idx = idx_ref[...]
    sel = (eidx == idx[0:1, :])
    for k in range(1, TOP_K):
        sel = sel | (eidx == idx[k:k + 1, :])
    self32 = jnp.where(sel, 1.0, 0.0)
    tile_cnt = jnp.sum(self32, axis=1, keepdims=True)

    @pl.when((phase == 0) & (i == 0))
    def _():
        cnt_ref[...] = jnp.zeros_like(cnt_ref)

    @pl.when(phase == 0)
    def _():
        cnt_ref[...] = cnt_ref[...] + tile_cnt

    @pl.when((phase == 1) & (i == 0))
    def _():
        cnt = cnt_ref[...]
        padded = jnp.floor((cnt + (blk - 1)) * (1.0 / blk)) * blk
        run = jnp.zeros((1, LANES), _F32)
        rows = []
        for e in range(N_EXPERTS):
            rows.append(run)
            run = run + padded[e:e + 1, :]
        start = jnp.concatenate(rows, axis=0)
        start_ref[...] = start
        run_ref[...] = start
        nbp = meta_ref.shape[1]
        s1 = start[:, 0:1]
        e1 = s1 + padded[:, 0:1]
        c1 = cnt[:, 0:1]
        row0 = (lax.broadcasted_iota(jnp.int32, (N_EXPERTS, nbp), 1) * blk).astype(_F32)
        owner = jnp.sum(jnp.where(e1 <= row0, 1.0, 0.0), axis=0, keepdims=True)
        owner = jnp.minimum(owner, N_EXPERTS - 1.0)
        inside = (s1 <= row0) & (row0 < e1)
        nval = jnp.sum(jnp.where(inside, jnp.clip(c1 - (row0 - s1), 0.0, blk), 0.0), axis=0, keepdims=True)
        meta_ref[...] = jnp.concatenate(
            [owner.astype(jnp.int32), nval.astype(jnp.int32), jnp.zeros((6, nbp), jnp.int32)], axis=0)

    @pl.when(phase == 1)
    def _():
        rank = _dot(self32.astype(_BF16), tri_ref[...])
        slot = (run_ref[:, 0:1] + rank).astype(jnp.int32)
        for k in range(TOP_K):
            sk = jnp.sum(jnp.where(eidx == idx[k:k + 1, :], slot, 0), axis=0, keepdims=True)
            slot_ref[0, :, k * t:(k + 1) * t] = sk
        run_ref[...] = run_ref[...] + tile_cnt


def _route(idx_t, tri_strict, n_blocks_pad):
    n = idx_t.shape[1]
    t = SEQ_TILE
    nt = n // t
    return pl.pallas_call(
        _route_kernel,
        grid=(2, nt),
        in_specs=[
            pl.BlockSpec((8, t), lambda ph, i: (0, i)),
            pl.BlockSpec((t, t), lambda ph, i: (0, 0)),
        ],
        out_specs=[
            pl.BlockSpec((1, 1, TOP_K * t), lambda ph, i: (ph * i, 0, 0)),
            pl.BlockSpec((8, n_blocks_pad), lambda ph, i: (0, 0)),
        ],
        out_shape=[
            jax.ShapeDtypeStruct((nt, 1, TOP_K * t), jnp.int32),
            jax.ShapeDtypeStruct((8, n_blocks_pad), jnp.int32),
        ],
        scratch_shapes=[
            pltpu.VMEM((N_EXPERTS, LANES), _F32),
            pltpu.VMEM((N_EXPERTS, LANES), _F32),
            pltpu.VMEM((N_EXPERTS, LANES), _F32),
        ],
        compiler_params=_params("arbitrary", "arbitrary"),
        name="route",
    )(idx_t, tri_strict)


def _row_copy(src_ref, src_row, dst_ref, dst_row, sem):
    return pltpu.make_async_copy(src_ref.at[pl.ds(src_row, 1), :], dst_ref.at[pl.ds(dst_row, 1), :], sem)


def _dispatch_kernel(meta_ref, slot_ref, h_ref, xs_ref, zero_buf, sem, zsem):
    t = h_ref.shape[0]
    blk = zero_buf.shape[0]

    @pl.when(pl.program_id(0) == 0)
    def _():
        zero_buf[...] = jnp.zeros_like(zero_buf)
        n_blocks = xs_ref.shape[0] // blk

        def fill(bi):
            return pltpu.make_async_copy(zero_buf, xs_ref.at[pl.ds(pl.multiple_of(bi * blk, blk), blk), :], zsem)

        def start(bi, carry):
            @pl.when(meta_ref[1, bi] < blk)
            def _():
                fill(bi).start()
            return carry

        def wait(bi, carry):
            @pl.when(meta_ref[1, bi] < blk)
            def _():
                fill(bi).wait()
            return carry

        lax.fori_loop(0, n_blocks, start, 0)
        lax.fori_loop(0, n_blocks, wait, 0)

    def issue(r, carry):
        for k in range(TOP_K):
            _row_copy(h_ref, r, xs_ref, slot_ref[0, 0, k * t + r], sem).start()
        return carry

    lax.fori_loop(0, t, issue, 0)

    def drain(r, carry):
        for k in range(TOP_K):
            _row_copy(h_ref, r, xs_ref, slot_ref[0, 0, k * t + r], sem).wait()
        return carry

    lax.fori_loop(0, t, drain, 0)


def _dispatch(meta, slots, h, n_rows):
    n, d = h.shape
    t = SEQ_TILE
    grid_spec = pltpu.PrefetchScalarGridSpec(
        num_scalar_prefetch=1,
        grid=(n // t,),
        in_specs=[
            pl.BlockSpec((1, 1, TOP_K * t), lambda i, m: (i, 0, 0), memory_space=pltpu.SMEM),
            pl.BlockSpec((t, d), lambda i, m: (i, 0)),
        ],
        out_specs=pl.BlockSpec(memory_space=pl.ANY),
        scratch_shapes=[pltpu.VMEM((EXPERT_ROWS, d), _F32), pltpu.SemaphoreType.DMA(()),
                        pltpu.SemaphoreType.DMA(())],
    )
    return pl.pallas_call(
        _dispatch_kernel,
        grid_spec=grid_spec,
        out_shape=jax.ShapeDtypeStruct((n_rows, d), _F32),
        compiler_params=_params("arbitrary"),
        name="dispatch",
    )(meta, slots, h)


def _expert_kernel(meta_ref, x_ref, wgu_ref, bgu_ref, wd_ref, bd_ref, y_ref, wgu_bf, wd_bf):
    b = pl.program_id(0)
    e = meta_ref[0, b]
    nval = meta_ref[1, b]
    prev = meta_ref[0, jnp.maximum(b - 1, 0)]
    f = wd_ref.shape[1]

    @pl.when((b == 0) | (prev != e))
    def _():
        wgu_bf[...] = wgu_ref[0].astype(_BF16)
        wd_bf[...] = wd_ref[0].astype(_BF16)

    @pl.when(nval > 0)
    def _():
        x = x_ref[...].astype(_BF16)
        hgu = _dot(x, wgu_bf[...]) + bgu_ref[0]
        gate = jnp.minimum(hgu[:, :f], SWIGLU_LIMIT)
        up = jnp.clip(hgu[:, f:], -SWIGLU_LIMIT, SWIGLU_LIMIT)
        act = gate * jax.nn.sigmoid(gate * SWIGLU_ALPHA) * (up + 1.0)
        y_ref[...] = _dot(act.astype(_BF16), wd_bf[...]) + bd_ref[0]

    @pl.when(nval <= 0)
    def _():
        y_ref[...] = jnp.zeros_like(y_ref)


def _experts(meta, xs, w_gu, b_gu, w_d, b_d):
    r, d = xs.shape
    ne, _, f2 = w_gu.shape
    f = w_d.shape[1]
    blk = EXPERT_ROWS
    grid_spec = pltpu.PrefetchScalarGridSpec(
        num_scalar_prefetch=1,
        grid=(r // blk,),
        in_specs=[
            pl.BlockSpec((blk, d), lambda i, m: (i, 0)),
            pl.BlockSpec((1, d, f2), lambda i, m: (m[0, i], 0, 0)),
            pl.BlockSpec((1, 1, f2), lambda i, m: (m[0, i], 0, 0)),
            pl.BlockSpec((1, f, d), lambda i, m: (m[0, i], 0, 0)),
            pl.BlockSpec((1, 1, d), lambda i, m: (m[0, i], 0, 0)),
        ],
        out_specs=pl.BlockSpec((blk, d), lambda i, m: (i, 0)),
        scratch_shapes=[pltpu.VMEM((d, f2), _BF16), pltpu.VMEM((f, d), _BF16)],
    )
    return pl.pallas_call(
        _expert_kernel,
        grid_spec=grid_spec,
        out_shape=jax.ShapeDtypeStruct((r, d), _F32),
        compiler_params=_params("arbitrary"),
        name="experts",
    )(meta, xs, w_gu, b_gu.reshape(ne, 1, f2), w_d, b_d.reshape(ne, 1, d))


def _combine_kernel(alpha, slot_ref, h_ref, gate_ref, g2_ref, b2_ref, ys_ref, o_ref, buf, sem):
    t = h_ref.shape[0]

    def issue(r, carry):
        for k in range(TOP_K):
            _row_copy(ys_ref, slot_ref[0, 0, k * t + r], buf.at[k], r, sem).start()
        return carry

    lax.fori_loop(0, t, issue, 0)

    def drain(r, carry):
        for k in range(TOP_K):
            _row_copy(ys_ref, slot_ref[0, 0, k * t + r], buf.at[k], r, sem).wait()
        return carry

    lax.fori_loop(0, t, drain, 0)

    gate = gate_ref[...]
    moe = gate[:, 0:1] * buf[0]
    for k in range(1, TOP_K):
        moe = moe + gate[:, k:k + 1] * buf[k]
    o_ref[...] = _layer_norm(alpha * h_ref[...] + moe, g2_ref[...], b2_ref[...])


def _combine(alpha, slots, h, gate, g2, b2, ys):
    n, d = h.shape
    t = SEQ_TILE
    return pl.pallas_call(
        functools.partial(_combine_kernel, alpha),
        grid=(n // t,),
        in_specs=[
            pl.BlockSpec((1, 1, TOP_K * t), lambda i: (i, 0, 0), memory_space=pltpu.SMEM),
            pl.BlockSpec((t, d), lambda i: (i, 0)),
            pl.BlockSpec((t, LANES), lambda i: (i, 0)),
            pl.BlockSpec((1, d), lambda i: (0, 0)),
            pl.BlockSpec((1, d), lambda i: (0, 0)),
            pl.BlockSpec(memory_space=pl.ANY),
        ],
        out_specs=pl.BlockSpec((t, d), lambda i: (i, 0)),
        out_shape=jax.ShapeDtypeStruct((n, d), _F32),
        scratch_shapes=[pltpu.VMEM((TOP_K, t, d), _F32), pltpu.SemaphoreType.DMA(())],
        compiler_params=_params("arbitrary"),
        name="combine",
    )(slots, h, gate, g2, b2, ys)


def _pad_cols(a, width):
    return jnp.pad(a, ((0, 0), (0, width - a.shape[1])))


def kernel(x, w_in, b_forget, sink, w_proj_swa, w_proj_fox, w_out, ln1_g, ln1_b, w_router, b_router,
           w_gate_up, b_gate_up, w_down, b_down, ln2_g, ln2_b):
    b, s, d = x.shape
    depth = w_in.shape[0]
    n = b * s
    alpha = float((2.0 * depth) ** 0.25)
    assert s % SEQ_TILE == 0 and SEQ_TILE % WINDOW == 0 and d % LANES == 0
    assert w_router.shape[2] == N_EXPERTS and w_in.shape[2] == ATT_W + FOX_HEADS + 2 * d

    blk = EXPERT_ROWS
    n_rows = ((n * TOP_K + blk - 1) // blk) * blk + N_EXPERTS * blk
    n_blocks_pad = ((n_rows // blk + LANES - 1) // LANES) * LANES
    t = SEQ_TILE
    ri = lax.broadcasted_iota(jnp.int32, (t, t), 0)
    ci = lax.broadcasted_iota(jnp.int32, (t, t), 1)
    tri_incl = (ci <= ri).astype(_BF16)
    tri_strict = (ri < ci).astype(_BF16)

    h = x
    for layer in range(depth):
        w_l = w_in[layer]
        w_att = _pad_cols(w_l[:, :ATT_W + FOX_HEADS], ATT_W + LANES).astype(_BF16)
        w_gates = w_l[:, ATT_W + FOX_HEADS:].astype(_BF16)
        bf_pad = _pad_cols(b_forget[layer].reshape(1, FOX_HEADS).astype(_F32), LANES)

        qkv, ccol, crow = _in_proj(h, w_att, bf_pad, tri_incl)
        att_a = _swa(qkv, sink[layer].reshape(-1).astype(_F32))
        att_f = _fox(qkv, ccol, crow)

        h1, idx_t, gate = _mix(
            alpha, h.reshape(n, d), att_a.reshape(n, SWA_Q_W), att_f.reshape(n, FOX_W),
            w_gates, w_proj_swa[layer].astype(_BF16), w_proj_fox[layer].astype(_BF16),
            w_out[layer].astype(_BF16), ln1_g[layer].reshape(1, d), ln1_b[layer].reshape(1, d),
            _pad_cols(w_router[layer], LANES).astype(_BF16),
            _pad_cols(b_router[layer].reshape(1, N_EXPERTS).astype(_F32), LANES))

        slots, meta = _route(idx_t, tri_strict, n_blocks_pad)
        xs = _dispatch(meta, slots, h1, n_rows)
        ys = _experts(meta, xs, w_gate_up[layer], b_gate_up[layer], w_down[layer], b_down[layer])
        out = _combine(alpha, slots, h1, gate, ln2_g[layer].reshape(1, d), ln2_b[layer].reshape(1, d), ys)
        h = out.reshape(b, s, d)
    return h
```

```python
import functools

import numpy as np
import jax
import jax.numpy as jnp
from jax import lax
from jax.experimental import pallas as pl
from jax.experimental.pallas import tpu as pltpu

HEAD_DIM = 64
SWA_Q_HEADS = 8
SWA_KV_HEADS = 2
SWA_GROUP = SWA_Q_HEADS // SWA_KV_HEADS
WINDOW = 128
FOX_HEADS = 8
N_EXPERTS = 32
TOP_K = 4
SWIGLU_LIMIT = 7.0
SWIGLU_ALPHA = 1.702
LN_EPS = 1e-5

LANES = 128
SEQ_TILE = 512
EXPERT_ROWS = 256
FOX_ROW_CHUNK = 64
VMEM_LIMIT = 56 * 1024 * 1024

SWA_Q_W = SWA_Q_HEADS * HEAD_DIM
SWA_KV_W = SWA_KV_HEADS * HEAD_DIM
FOX_W = FOX_HEADS * HEAD_DIM
ATT_W = SWA_Q_W + 2 * SWA_KV_W + 3 * FOX_W
NEG_INF = float("-inf")

_F32 = jnp.float32
_BF16 = jnp.bfloat16


def _dot(a, b):
    return jnp.dot(a, b, preferred_element_type=_F32)


def _dot_nt(a, b):
    return lax.dot_general(a, b, (((1,), (1,)), ((), ())), preferred_element_type=_F32)


def _params(*sem):
    return pltpu.CompilerParams(dimension_semantics=sem, vmem_limit_bytes=VMEM_LIMIT)


def _layer_norm(z, g, b):
    mu = jnp.mean(z, axis=-1, keepdims=True)
    zc = z - mu
    var = jnp.mean(zc * zc, axis=-1, keepdims=True)
    return zc * lax.rsqrt(var + LN_EPS) * g + b


def _split3(v):
    p1 = v.astype(_BF16).astype(_F32)
    r1 = v - p1
    p2 = r1.astype(_BF16).astype(_F32)
    p3 = (r1 - p2).astype(_BF16).astype(_F32)
    return p1, p2, p3


def _in_proj_kernel(x_ref, w_ref, bf_ref, tri_ref, qkv_ref, ccol_ref, kaug_ref, carry_ref):
    @pl.when(pl.program_id(1) == 0)
    def _():
        carry_ref[...] = jnp.zeros_like(carry_ref)

    xb = x_ref[0].astype(_BF16)
    acc = _dot(xb, w_ref[...])
    qkv_ref[0] = acc[:, :ATT_W].astype(_BF16)
    kf_col = SWA_Q_W + 2 * SWA_KV_W + FOX_W
    kt = acc[:, kf_col:kf_col + FOX_W].T
    z = acc[:, ATT_W:] + bf_ref[...]
    log_f = jnp.minimum(z, 0.0) - jnp.log1p(jnp.exp(-jnp.abs(z)))
    tri = tri_ref[...]
    p1 = log_f.astype(_BF16)
    r1 = log_f - p1.astype(_F32)
    p2 = r1.astype(_BF16)
    p3 = (r1 - p2.astype(_F32)).astype(_BF16)
    c = _dot(tri, p1) + _dot(tri, p2) + _dot(tri, p3) + carry_ref[0:1, :]
    t = c.shape[0]
    carry_ref[...] = jnp.broadcast_to(c[t - 1:t, :], carry_ref.shape)
    ccol_ref[0] = c
    ct = c.T
    row = lax.broadcasted_iota(jnp.int32, (8, t), 0)
    ones_rows = jnp.where(row < 3, 1.0, 0.0)
    pad_rows = jnp.zeros((HEAD_DIM - 16, t), _F32)
    for h in range(FOX_HEADS):
        c1, c2, c3 = _split3(ct[h:h + 1, :])
        ck_rows = jnp.where(row == 0, -c1, jnp.where(row == 1, -c2, jnp.where(row == 2, -c3, 0.0)))
        bias_half = jnp.concatenate([ones_rows, ck_rows, pad_rows], axis=0)
        kh = kt[h * HEAD_DIM:(h + 1) * HEAD_DIM, :]
        halves = [kh, bias_half] if h % 2 == 0 else [bias_half, kh]
        kaug_ref[0, h, 0] = jnp.concatenate(halves, axis=0).astype(_BF16)


def _in_proj(x, w_att, bf_pad, tri):
    b, s, d = x.shape
    t = SEQ_TILE
    wn = w_att.shape[1]
    return pl.pallas_call(
        _in_proj_kernel,
        grid=(b, s // t),
        in_specs=[
            pl.BlockSpec((1, t, d), lambda i, j: (i, j, 0)),
            pl.BlockSpec((d, wn), lambda i, j: (0, 0)),
            pl.BlockSpec((1, LANES), lambda i, j: (0, 0)),
            pl.BlockSpec((t, t), lambda i, j: (0, 0)),
        ],
        out_specs=[
            pl.BlockSpec((1, t, ATT_W), lambda i, j: (i, j, 0)),
            pl.BlockSpec((1, t, LANES), lambda i, j: (i, j, 0)),
            pl.BlockSpec((1, FOX_HEADS, 1, LANES, t), lambda i, j: (i, 0, j, 0, 0)),
        ],
        out_shape=[
            jax.ShapeDtypeStruct((b, s, ATT_W), _BF16),
            jax.ShapeDtypeStruct((b, s, LANES), _F32),
            jax.ShapeDtypeStruct((b, FOX_HEADS, s // t, LANES, t), _BF16),
        ],
        scratch_shapes=[pltpu.VMEM((8, LANES), _F32)],
        compiler_params=_params("arbitrary", "arbitrary"),
        name="in_proj",
    )(x, w_att, bf_pad, tri)


def _alibi_slopes(n):
    return [float(v) for v in np.asarray(2.0 ** (-8.0 * np.arange(1, n + 1) / n), dtype=np.float32)]


def _swa_kernel(sink_ref, q_ref, kp_ref, kc_ref, vp_ref, vc_ref, o_ref):
    blk = pl.program_id(1)
    w = WINDOW
    q = q_ref[0] * jnp.asarray(HEAD_DIM ** -0.5, _BF16)
    k = jnp.concatenate([kp_ref[0], kc_ref[0]], axis=0)
    v = jnp.concatenate([vp_ref[0], vc_ref[0]], axis=0)
    row = lax.broadcasted_iota(jnp.int32, (w, 2 * w), 0)
    col = lax.broadcasted_iota(jnp.int32, (w, 2 * w), 1)
    dist = row + w - col
    valid = (dist >= 0) & (dist < w) & ((col >= w) | (blk > 0))
    distf = dist.astype(_F32)
    slopes = _alibi_slopes(SWA_Q_HEADS)
    outs = []
    for h in range(SWA_Q_HEADS):
        g = h // SWA_GROUP
        qh = q[:, h * HEAD_DIM:(h + 1) * HEAD_DIM]
        kg = k[:, g * HEAD_DIM:(g + 1) * HEAD_DIM]
        vg = v[:, g * HEAD_DIM:(g + 1) * HEAD_DIM]
        sc = _dot_nt(qh, kg) - slopes[h] * distf
        sc = jnp.where(valid, sc, NEG_INF)
        sk = sink_ref[h]
        m = jnp.maximum(jnp.max(sc, axis=1, keepdims=True), sk)
        p = jnp.exp(sc - m)
        denom = jnp.sum(p, axis=1, keepdims=True) + jnp.exp(sk - m)
        o = _dot(p.astype(_BF16), vg)
        outs.append(o / denom)
    o_ref[0] = jnp.concatenate(outs, axis=1).astype(_BF16)


def _swa(qkv, sink_flat):
    b, s, _ = qkv.shape
    w = WINDOW
    kcol = SWA_Q_W // LANES
    vcol = kcol + SWA_KV_W // LANES
    grid_spec = pltpu.PrefetchScalarGridSpec(
        num_scalar_prefetch=1,
        grid=(b, s // w),
        in_specs=[
            pl.BlockSpec((1, w, SWA_Q_W), lambda i, j, sk: (i, j, 0)),
            pl.BlockSpec((1, w, SWA_KV_W), lambda i, j, sk: (i, jnp.maximum(j - 1, 0), kcol)),
            pl.BlockSpec((1, w, SWA_KV_W), lambda i, j, sk: (i, j, kcol)),
            pl.BlockSpec((1, w, SWA_KV_W), lambda i, j, sk: (i, jnp.maximum(j - 1, 0), vcol)),
            pl.BlockSpec((1, w, SWA_KV_W), lambda i, j, sk: (i, j, vcol)),
        ],
        out_specs=pl.BlockSpec((1, w, SWA_Q_W), lambda i, j, sk: (i, j, 0)),
    )
    return pl.pallas_call(
        _swa_kernel,
        grid_spec=grid_spec,
        out_shape=jax.ShapeDtypeStruct((b, s, SWA_Q_W), _BF16),
        compiler_params=_params("arbitrary", "arbitrary"),
        name="swa",
    )(sink_flat, qkv, qkv, qkv, qkv, qkv)


def _fox_kernel(q_ref, kaug_ref, v_ref, ccol_ref, o_ref, m_ref, l_ref, acc_ref, s_ref, p_ref):
    pair = pl.program_id(1)
    qi = pl.program_id(2)
    t = q_ref.shape[1]
    lane = lax.broadcasted_iota(jnp.int32, (t, LANES), 1)
    low = lane < HEAD_DIM
    q = (q_ref[0] * jnp.asarray(HEAD_DIM ** -0.5, _BF16)).astype(_F32)
    ccol = ccol_ref[0]
    qa = []
    for hh in range(2):
        cq = jnp.sum(jnp.where(lane == 2 * pair + hh, ccol, 0.0), axis=1, keepdims=True)
        c1, c2, c3 = _split3(cq)
        base = HEAD_DIM if hh == 0 else 0
        bias = jnp.where(lane == base, c1, jnp.where(lane == base + 1, c2, jnp.where(lane == base + 2, c3, 0.0)))
        bias = jnp.where((lane >= base + 8) & (lane < base + 11), 1.0, bias)
        own = low if hh == 0 else ~low
        qa.append(jnp.where(own, q, bias).astype(_BF16))

    m_ref[...] = jnp.full_like(m_ref, NEG_INF)
    l_ref[...] = jnp.zeros_like(l_ref)
    acc_ref[...] = jnp.zeros_like(acc_ref)
    rc = FOX_ROW_CHUNK
    low_c = lax.broadcasted_iota(jnp.int32, (rc, LANES), 1) < HEAD_DIM

    def logits(slot, j):
        for hh in range(2):
            s_ref[slot, hh] = _dot(qa[hh], kaug_ref[0, hh, j])

    def absorb(slot, j, masked):
        start = pl.multiple_of(j * t, t)
        v = v_ref[0, pl.ds(start, t), :]
        zero = jnp.zeros_like(v)
        vcat = jnp.concatenate([jnp.where(low, v, zero), jnp.where(low, zero, v)], axis=0)
        for r in range(t // rc):
            rows = pl.ds(r * rc, rc)
            if masked:
                keep = (lax.broadcasted_iota(jnp.int32, (rc, t), 1)
                        <= lax.broadcasted_iota(jnp.int32, (rc, t), 0) + r * rc)
            alphas = []
            for hh in range(2):
                s = s_ref[slot, hh, rows, :]
                if masked:
                    s = jnp.where(keep, s, NEG_INF)
                m_old = m_ref[hh, rows, :]
                m_new = jnp.maximum(m_old, jnp.max(s, axis=1, keepdims=True))
                alpha = jnp.exp(m_old - m_new)
                p = jnp.exp(s - jnp.tile(m_new, (1, t // LANES)))
                l_ref[hh, rows, :] = alpha * l_ref[hh, rows, :] + jnp.sum(p, axis=1, keepdims=True)
                m_ref[hh, rows, :] = m_new
                p_ref[rows, hh * t:(hh + 1) * t] = p.astype(_BF16)
                alphas.append(alpha)
            acc_ref[rows, :] = jnp.where(low_c, alphas[0], alphas[1]) * acc_ref[rows, :]
        acc_ref[...] = acc_ref[...] + _dot(p_ref[...], vcat)

    logits(0, 0)

    def body(i, carry):
        logits(1, 2 * i + 1)
        absorb(0, 2 * i, False)
        logits(0, 2 * i + 2)
        absorb(1, 2 * i + 1, False)
        return carry

    lax.fori_loop(0, qi // 2, body, 0)

    @pl.when(qi % 2 == 0)
    def _():
        absorb(0, qi, True)

    @pl.when(qi % 2 == 1)
    def _():
        logits(1, qi)
        absorb(0, qi - 1, False)
        absorb(1, qi, True)

    o_ref[0] = (acc_ref[...] / jnp.where(low, l_ref[0], l_ref[1])).astype(_BF16)


def _fox(qkv, ccol, kaug):
    b, s, _ = qkv.shape
    t = SEQ_TILE
    pairs = FOX_HEADS // 2
    qcol = (SWA_Q_W + 2 * SWA_KV_W) // LANES
    vcol = qcol + 2 * FOX_W // LANES
    return pl.pallas_call(
        _fox_kernel,
        grid=(b, pairs, s // t),
        in_specs=[
            pl.BlockSpec((1, t, LANES), lambda i, p, j: (i, j, qcol + p)),
            pl.BlockSpec((1, 2, s // t, LANES, t), lambda i, p, j: (i, p, 0, 0, 0)),
            pl.BlockSpec((1, s, LANES), lambda i, p, j: (i, 0, vcol + p)),
            pl.BlockSpec((1, t, LANES), lambda i, p, j: (i, j, 0)),
        ],
        out_specs=pl.BlockSpec((1, t, LANES), lambda i, p, j: (i, j, p)),
        out_shape=jax.ShapeDtypeStruct((b, s, FOX_W), _BF16),
        scratch_shapes=[
            pltpu.VMEM((2, t, LANES), _F32),
            pltpu.VMEM((2, t, LANES), _F32),
            pltpu.VMEM((t, LANES), _F32),
            pltpu.VMEM((2, 2, t, t), _F32),
            pltpu.VMEM((t, 2 * t), _BF16),
        ],
        compiler_params=_params("arbitrary", "arbitrary", "arbitrary"),
        name="fox",
    )(qkv, kaug, qkv, ccol)


def _mix_kernel(alpha, x_ref, aa_ref, af_ref, wg_ref, wps_ref, wpf_ref, wo_ref, g1_ref, b1_ref,
                wr_ref, br_ref, h_ref, idx_ref, gate_ref):
    d = x_ref.shape[1]
    t = x_ref.shape[0]
    x = x_ref[...]
    gates = _dot(x.astype(_BF16), wg_ref[...])
    ya = _dot(aa_ref[...], wps_ref[...])
    yf = _dot(af_ref[...], wpf_ref[...])
    mix = jax.nn.sigmoid(gates[:, :d]) * ya + jax.nn.sigmoid(gates[:, d:]) * yf
    z = alpha * x + _dot(mix.astype(_BF16), wo_ref[...])
    h = _layer_norm(z, g1_ref[...], b1_ref[...])
    h_ref[...] = h

    logits = _dot(h.astype(_BF16), wr_ref[...]) + br_ref[...]
    lt = logits.T[:N_EXPERTS, :]
    eidx = lax.broadcasted_iota(jnp.int32, (N_EXPERTS, t), 0)
    work = lt
    vals, idxs = [], []
    for _ in range(TOP_K):
        mk = jnp.max(work, axis=0, keepdims=True)
        ik = jnp.min(jnp.where(work == mk, eidx, N_EXPERTS), axis=0, keepdims=True)
        vals.append(mk)
        idxs.append(ik)
        work = jnp.where(eidx == ik, NEG_INF, work)
    ex = [jnp.exp(vk - vals[0]) for vk in vals]
    denom = ex[0] + ex[1] + ex[2] + ex[3]
    idx_ref[...] = jnp.concatenate(idxs + [jnp.zeros((8 - TOP_K, t), jnp.int32)], axis=0)
    gt = jnp.concatenate([e / denom for e in ex] + [jnp.zeros((LANES - TOP_K, t), _F32)], axis=0)
    gate_ref[...] = gt.T


def _mix(alpha, x2, att_a, att_f, wg, wps, wpf, wo, g1, b1, wr, br):
    n, d = x2.shape
    t = SEQ_TILE
    const = lambda shape: pl.BlockSpec(shape, lambda i: (0, 0))
    return pl.pallas_call(
        functools.partial(_mix_kernel, alpha),
        grid=(n // t,),
        in_specs=[
            pl.BlockSpec((t, d), lambda i: (i, 0)),
            pl.BlockSpec((t, SWA_Q_W), lambda i: (i, 0)),
            pl.BlockSpec((t, FOX_W), lambda i: (i, 0)),
            const(wg.shape), const(wps.shape), const(wpf.shape), const(wo.shape),
            const(g1.shape), const(b1.shape), const(wr.shape), const(br.shape),
        ],
        out_specs=[
            pl.BlockSpec((t, d), lambda i: (i, 0)),
            pl.BlockSpec((8, t), lambda i: (0, i)),
            pl.BlockSpec((t, LANES), lambda i: (i, 0)),
        ],
        out_shape=[
            jax.ShapeDtypeStruct((n, d), _F32),
            jax.ShapeDtypeStruct((8, n), jnp.int32),
            jax.ShapeDtypeStruct((n, LANES), _F32),
        ],
        compiler_params=_params("arbitrary"),
        name="mix",
    )(x2, att_a, att_f, wg, wps, wpf, wo, g1, b1, wr, br)


def _route_kernel(idx_ref, tri_ref, slot_ref, meta_ref, cnt_ref, start_ref, run_ref):
    phase = pl.program_id(0)
    i = pl.program_id(1)
    t = idx_ref.shape[1]
    blk = EXPERT_ROWS
    eidx = lax.broadcasted_iota(jnp.int32, (N_EXPERTS, t), 0)
    idx = idx_ref[...]
    sel = (eidx == idx[0:1, :])
    for k in range(1, TOP_K):
        sel = sel | (eidx == idx[k:k + 1, :])
    self32 = jnp.where(sel, 1.0, 0.0)
    tile_cnt = jnp.sum(self32, axis=1, keepdims=True)

    @pl.when((phase == 0) & (i == 0))
    def _():
        cnt_ref[...] = jnp.zeros_like(cnt_ref)

    @pl.when(phase == 0)
    def _():
        cnt_ref[...] = cnt_ref[...] + tile_cnt

    @pl.when((phase == 1) & (i == 0))
    def _():
        cnt = cnt_ref[...]
        padded = jnp.floor((cnt + (blk - 1)) * (1.0 / blk)) * blk
        run = jnp.zeros((1, LANES), _F32)
        rows = []
        for e in range(N_EXPERTS):
            rows.append(run)
            run = run + padded[e:e + 1, :]
        start = jnp.concatenate(rows, axis=0)
        start_ref[...] = start
        run_ref[...] = start
        nbp = meta_ref.shape[1]
        s1 = start[:, 0:1]
        e1 = s1 + padded[:, 0:1]
        c1 = cnt[:, 0:1]
        row0 = (lax.broadcasted_iota(jnp.int32, (N_EXPERTS, nbp), 1) * blk).astype(_F32)
        owner = jnp.sum(jnp.where(e1 <= row0, 1.0, 0.0), axis=0, keepdims=True)
        owner = jnp.minimum(owner, N_EXPERTS - 1.0)
        inside = (s1 <= row0) & (row0 < e1)
        nval = jnp.sum(jnp.where(inside, jnp.clip(c1 - (row0 - s1), 0.0, blk), 0.0), axis=0, keepdims=True)
        erow = lax.broadcasted_iota(jnp.int32, (N_EXPERTS, nbp), 0).astype(_F32)
        later = (erow > owner) & (c1 > 0.0)
        nxt = jnp.min(jnp.where(later, erow, float(N_EXPERTS)), axis=0, keepdims=True)
        meta_ref[...] = jnp.concatenate(
            [owner.astype(jnp.int32), nval.astype(jnp.int32), nxt.astype(jnp.int32),
             jnp.zeros((5, nbp), jnp.int32)], axis=0)

    @pl.when(phase == 1)
    def _():
        rank = _dot(self32.astype(_BF16), tri_ref[...])
        slot = (run_ref[:, 0:1] + rank).astype(jnp.int32)
        for k in range(TOP_K):
            sk = jnp.sum(jnp.where(eidx == idx[k:k + 1, :], slot, 0), axis=0, keepdims=True)
            slot_ref[0, :, k * t:(k + 1) * t] = sk
        run_ref[...] = run_ref[...] + tile_cnt


def _route(idx_t, tri_strict, n_blocks_pad):
    n = idx_t.shape[1]
    t = SEQ_TILE
    nt = n // t
    return pl.pallas_call(
        _route_kernel,
        grid=(2, nt),
        in_specs=[
            pl.BlockSpec((8, t), lambda ph, i: (0, i)),
            pl.BlockSpec((t, t), lambda ph, i: (0, 0)),
        ],
        out_specs=[
            pl.BlockSpec((1, 1, TOP_K * t), lambda ph, i: (ph * i, 0, 0)),
            pl.BlockSpec((8, n_blocks_pad), lambda ph, i: (0, 0)),
        ],
        out_shape=[
            jax.ShapeDtypeStruct((nt, 1, TOP_K * t), jnp.int32),
            jax.ShapeDtypeStruct((8, n_blocks_pad), jnp.int32),
        ],
        scratch_shapes=[
            pltpu.VMEM((N_EXPERTS, LANES), _F32),
            pltpu.VMEM((N_EXPERTS, LANES), _F32),
            pltpu.VMEM((N_EXPERTS, LANES), _F32),
        ],
        compiler_params=_params("arbitrary", "arbitrary"),
        name="route",
    )(idx_t, tri_strict)


def _row_copy(src_ref, src_row, dst_ref, dst_row, sem):
    return pltpu.make_async_copy(src_ref.at[pl.ds(src_row, 1), :], dst_ref.at[pl.ds(dst_row, 1), :], sem)


def _dispatch_kernel(meta_ref, slot_ref, h_ref, xs_ref, zero_buf, sem, zsem):
    t = h_ref.shape[0]
    blk = zero_buf.shape[0]

    @pl.when(pl.program_id(0) == 0)
    def _():
        zero_buf[...] = jnp.zeros_like(zero_buf)
        n_blocks = xs_ref.shape[0] // blk

        def fill(bi):
            return pltpu.make_async_copy(zero_buf, xs_ref.at[pl.ds(pl.multiple_of(bi * blk, blk), blk), :], zsem)

        def start(bi, carry):
            @pl.when(meta_ref[1, bi] < blk)
            def _():
                fill(bi).start()
            return carry

        def wait(bi, carry):
            @pl.when(meta_ref[1, bi] < blk)
            def _():
                fill(bi).wait()
            return carry

        lax.fori_loop(0, n_blocks, start, 0)
        lax.fori_loop(0, n_blocks, wait, 0)

    def issue(r, carry):
        for k in range(TOP_K):
            _row_copy(h_ref, r, xs_ref, slot_ref[0, 0, k * t + r], sem).start(priority=k % 2)
        return carry

    lax.fori_loop(0, t, issue, 0)

    def drain(r, carry):
        for k in range(TOP_K):
            _row_copy(h_ref, r, xs_ref, slot_ref[0, 0, k * t + r], sem).wait()
        return carry

    lax.fori_loop(0, t, drain, 0)


def _dispatch(meta, slots, h, n_rows):
    n, d = h.shape
    t = SEQ_TILE
    grid_spec = pltpu.PrefetchScalarGridSpec(
        num_scalar_prefetch=1,
        grid=(n // t,),
        in_specs=[
            pl.BlockSpec((1, 1, TOP_K * t), lambda i, m: (i, 0, 0), memory_space=pltpu.SMEM),
            pl.BlockSpec((t, d), lambda i, m: (i, 0)),
        ],
        out_specs=pl.BlockSpec(memory_space=pl.ANY),
        scratch_shapes=[pltpu.VMEM((EXPERT_ROWS, d), _F32), pltpu.SemaphoreType.DMA(()),
                        pltpu.SemaphoreType.DMA(())],
    )
    return pl.pallas_call(
        _dispatch_kernel,
        grid_spec=grid_spec,
        out_shape=jax.ShapeDtypeStruct((n_rows, d), _F32),
        compiler_params=_params("arbitrary"),
        name="dispatch",
    )(meta, slots, h)


def _expert_kernel(meta_ref, x_ref, wgu_hbm, bgu_ref, wd_hbm, bd_ref, y_ref,
                   wgu_land, wd_land, wgu_bf, wd_bf, sems, turn_ref):
    b = pl.program_id(0)
    e = meta_ref[0, b]
    nval = meta_ref[1, b]
    prev = meta_ref[0, jnp.maximum(b - 1, 0)]
    f = wd_bf.shape[0]

    def fetch(expert, slot):
        return (pltpu.make_async_copy(wgu_hbm.at[expert], wgu_land.at[slot], sems.at[0, slot]),
                pltpu.make_async_copy(wd_hbm.at[expert], wd_land.at[slot], sems.at[1, slot]))

    @pl.when(b == 0)
    def _():
        turn_ref[0] = 0
        for cp in fetch(e, 0):
            cp.start()

    @pl.when((nval > 0) & ((b == 0) | (prev != e)))
    def _():
        slot = turn_ref[0]
        for cp in fetch(e, slot):
            cp.wait()
        nxt = meta_ref[2, b]

        @pl.when(nxt < N_EXPERTS)
        def _():
            for cp in fetch(nxt, 1 - slot):
                cp.start()

        wgu_bf[...] = wgu_land[slot].astype(_BF16)
        wd_bf[...] = wd_land[slot].astype(_BF16)
        turn_ref[0] = 1 - slot

    @pl.when(nval > 0)
    def _():
        x = x_ref[...].astype(_BF16)
        hgu = _dot(x, wgu_bf[...]) + bgu_ref[0]
        gate = jnp.minimum(hgu[:, :f], SWIGLU_LIMIT)
        up = jnp.clip(hgu[:, f:], -SWIGLU_LIMIT, SWIGLU_LIMIT)
        act = gate * jax.nn.sigmoid(gate * SWIGLU_ALPHA) * (up + 1.0)
        y_ref[...] = _dot(act.astype(_BF16), wd_bf[...]) + bd_ref[0]

    @pl.when(nval <= 0)
    def _():
        y_ref[...] = jnp.zeros_like(y_ref)


def _experts(meta, xs, w_gu, b_gu, w_d, b_d):
    r, d = xs.shape
    ne, _, f2 = w_gu.shape
    f = w_d.shape[1]
    blk = EXPERT_ROWS
    grid_spec = pltpu.PrefetchScalarGridSpec(
        num_scalar_prefetch=1,
        grid=(r // blk,),
        in_specs=[
            pl.BlockSpec((blk, d), lambda i, m: (i, 0)),
            pl.BlockSpec(memory_space=pl.ANY),
            pl.BlockSpec((1, 1, f2), lambda i, m: (m[0, i], 0, 0)),
            pl.BlockSpec(memory_space=pl.ANY),
            pl.BlockSpec((1, 1, d), lambda i, m: (m[0, i], 0, 0)),
        ],
        out_specs=pl.BlockSpec((blk, d), lambda i, m: (i, 0)),
        scratch_shapes=[
            pltpu.VMEM((2, d, f2), _F32), pltpu.VMEM((2, f, d), _F32),
            pltpu.VMEM((d, f2), _BF16), pltpu.VMEM((f, d), _BF16),
            pltpu.SemaphoreType.DMA((2, 2)), pltpu.SMEM((1,), jnp.int32),
        ],
    )
    return pl.pallas_call(
        _expert_kernel,
        grid_spec=grid_spec,
        out_shape=jax.ShapeDtypeStruct((r, d), _F32),
        compiler_params=_params("arbitrary"),
        name="experts",
    )(meta, xs, w_gu, b_gu.reshape(ne, 1, f2), w_d, b_d.reshape(ne, 1, d))


def _combine_kernel(alpha, slot_ref, h_ref, gate_ref, g2_ref, b2_ref, ys_ref, o_ref, buf, sem):
    t = h_ref.shape[0]

    def issue(r, carry):
        for k in range(TOP_K):
            _row_copy(ys_ref, slot_ref[0, 0, k * t + r], buf.at[k], r, sem).start(priority=k % 2)
        return carry

    lax.fori_loop(0, t, issue, 0)

    def drain(r, carry):
        for k in range(TOP_K):
            _row_copy(ys_ref, slot_ref[0, 0, k * t + r], buf.at[k], r, sem).wait()
        return carry

    lax.fori_loop(0, t, drain, 0)

    gate = gate_ref[...]
    moe = gate[:, 0:1] * buf[0]
    for k in range(1, TOP_K):
        moe = moe + gate[:, k:k + 1] * buf[k]
    o_ref[...] = _layer_norm(alpha * h_ref[...] + moe, g2_ref[...], b2_ref[...])


def _combine(alpha, slots, h, gate, g2, b2, ys):
    n, d = h.shape
    t = SEQ_TILE
    return pl.pallas_call(
        functools.partial(_combine_kernel, alpha),
        grid=(n // t,),
        in_specs=[
            pl.BlockSpec((1, 1, TOP_K * t), lambda i: (i, 0, 0), memory_space=pltpu.SMEM),
            pl.BlockSpec((t, d), lambda i: (i, 0)),
            pl.BlockSpec((t, LANES), lambda i: (i, 0)),
            pl.BlockSpec((1, d), lambda i: (0, 0)),
            pl.BlockSpec((1, d), lambda i: (0, 0)),
            pl.BlockSpec(memory_space=pl.ANY),
        ],
        out_specs=pl.BlockSpec((t, d), lambda i: (i, 0)),
        out_shape=jax.ShapeDtypeStruct((n, d), _F32),
        scratch_shapes=[pltpu.VMEM((TOP_K, t, d), _F32), pltpu.SemaphoreType.DMA(())],
        compiler_params=_params("arbitrary"),
        name="combine",
    )(slots, h, gate, g2, b2, ys)


def _pad_cols(a, width):
    return jnp.pad(a, ((0, 0), (0, width - a.shape[1])))


def kernel(x, w_in, b_forget, sink, w_proj_swa, w_proj_fox, w_out, ln1_g, ln1_b, w_router, b_router,
           w_gate_up, b_gate_up, w_down, b_down, ln2_g, ln2_b):
    b, s, d = x.shape
    depth = w_in.shape[0]
    n = b * s
    alpha = float((2.0 * depth) ** 0.25)
    assert s % SEQ_TILE == 0 and SEQ_TILE % WINDOW == 0 and d % LANES == 0
    assert w_router.shape[2] == N_EXPERTS and w_in.shape[2] == ATT_W + FOX_HEADS + 2 * d

    blk = EXPERT_ROWS
    n_rows = ((n * TOP_K + blk - 1) // blk) * blk + N_EXPERTS * blk
    n_blocks_pad = ((n_rows // blk + LANES - 1) // LANES) * LANES
    t = SEQ_TILE
    ri = lax.broadcasted_iota(jnp.int32, (t, t), 0)
    ci = lax.broadcasted_iota(jnp.int32, (t, t), 1)
    tri_incl = (ci <= ri).astype(_BF16)
    tri_strict = (ri < ci).astype(_BF16)

    h = x
    for layer in range(depth):
        w_l = w_in[layer]
        w_att = _pad_cols(w_l[:, :ATT_W + FOX_HEADS], ATT_W + LANES).astype(_BF16)
        w_gates = w_l[:, ATT_W + FOX_HEADS:].astype(_BF16)
        bf_pad = _pad_cols(b_forget[layer].reshape(1, FOX_HEADS).astype(_F32), LANES)

        qkv, ccol, kaug = _in_proj(h, w_att, bf_pad, tri_incl)
        att_a = _swa(qkv, sink[layer].reshape(-1).astype(_F32))
        att_f = _fox(qkv, ccol, kaug)

        h1, idx_t, gate = _mix(
            alpha, h.reshape(n, d), att_a.reshape(n, SWA_Q_W), att_f.reshape(n, FOX_W),
            w_gates, w_proj_swa[layer].astype(_BF16), w_proj_fox[layer].astype(_BF16),
            w_out[layer].astype(_BF16), ln1_g[layer].reshape(1, d), ln1_b[layer].reshape(1, d),
            _pad_cols(w_router[layer], LANES).astype(_BF16),
            _pad_cols(b_router[layer].reshape(1, N_EXPERTS).astype(_F32), LANES))

        slots, meta = _route(idx_t, tri_strict, n_blocks_pad)
        xs = _dispatch(meta, slots, h1, n_rows)
        ys = _experts(meta, xs, w_gate_up[layer], b_gate_up[layer], w_down[layer], b_down[layer])
        out = _combine(alpha, slots, h1, gate, ln2_g[layer].reshape(1, d), ln2_b[layer].reshape(1, d), ys)
        h = out.reshape(b, s, d)
    return h
```

```python
import functools

import numpy as np
import jax
import jax.numpy as jnp
from jax import lax
from jax.experimental import pallas as pl
from jax.experimental.pallas import tpu as pltpu

HEAD_DIM = 64
SWA_Q_HEADS = 8
SWA_KV_HEADS = 2
SWA_GROUP = SWA_Q_HEADS // SWA_KV_HEADS
WINDOW = 128
FOX_HEADS = 8
N_EXPERTS = 32
TOP_K = 4
SWIGLU_LIMIT = 7.0
SWIGLU_ALPHA = 1.702
LN_EPS = 1e-5

LANES = 128
SEQ_TILE = 512
EXPERT_ROWS = 256
FOX_ROW_CHUNK = 64
VMEM_LIMIT = 56 * 1024 * 1024

SWA_Q_W = SWA_Q_HEADS * HEAD_DIM
SWA_KV_W = SWA_KV_HEADS * HEAD_DIM
FOX_W = FOX_HEADS * HEAD_DIM
ATT_W = SWA_Q_W + 2 * SWA_KV_W + 3 * FOX_W
NEG_INF = float("-inf")
LOG2E = 1.4426950408889634
FOX_Q_SCALE = HEAD_DIM ** -0.5 * LOG2E

_F32 = jnp.float32
_BF16 = jnp.bfloat16


def _dot(a, b):
    return jnp.dot(a, b, preferred_element_type=_F32)


def _dot_nt(a, b):
    return lax.dot_general(a, b, (((1,), (1,)), ((), ())), preferred_element_type=_F32)


def _params(*sem):
    return pltpu.CompilerParams(dimension_semantics=sem, vmem_limit_bytes=VMEM_LIMIT)


def _layer_norm(z, g, b):
    mu = jnp.mean(z, axis=-1, keepdims=True)
    zc = z - mu
    var = jnp.mean(zc * zc, axis=-1, keepdims=True)
    return zc * lax.rsqrt(var + LN_EPS) * g + b


def _store_slabs(ref, value, lead=()):
    rows, width = value.shape
    c = width // LANES
    for j in range(c):
        ref[lead + (pl.ds(j, rows, stride=c), slice(None))] = value[:, j * LANES:(j + 1) * LANES]


def _load_slabs(ref, rows, c, lead=()):
    return jnp.concatenate([ref[lead + (pl.ds(j, rows, stride=c), slice(None))] for j in range(c)], axis=1)


def _split3(v):
    p1 = v.astype(_BF16).astype(_F32)
    r1 = v - p1
    p2 = r1.astype(_BF16).astype(_F32)
    p3 = (r1 - p2).astype(_BF16).astype(_F32)
    return p1, p2, p3


def _in_proj_kernel(x_ref, w_ref, bf_ref, tri_ref, qkv_ref, ccol_ref, kaug_ref, carry_ref):
    @pl.when(pl.program_id(1) == 0)
    def _():
        carry_ref[...] = jnp.zeros_like(carry_ref)

    xb = x_ref[0].astype(_BF16)
    acc = _dot(xb, w_ref[...])
    qf_col = SWA_Q_W + 2 * SWA_KV_W
    qkv_ref[0, :, :qf_col] = acc[:, :qf_col].astype(_BF16)
    qkv_ref[0, :, qf_col:qf_col + FOX_W] = (acc[:, qf_col:qf_col + FOX_W] * FOX_Q_SCALE).astype(_BF16)
    qkv_ref[0, :, qf_col + FOX_W:] = acc[:, qf_col + FOX_W:ATT_W].astype(_BF16)
    kf_col = SWA_Q_W + 2 * SWA_KV_W + FOX_W
    kt = acc[:, kf_col:kf_col + FOX_W].T
    z = acc[:, ATT_W:] + bf_ref[...]
    log_f = jnp.minimum(z, 0.0) - jnp.log1p(jnp.exp(-jnp.abs(z)))
    tri = tri_ref[...]
    p1 = log_f.astype(_BF16)
    r1 = log_f - p1.astype(_F32)
    p2 = r1.astype(_BF16)
    p3 = (r1 - p2.astype(_F32)).astype(_BF16)
    c = _dot(tri, p1) + _dot(tri, p2) + _dot(tri, p3) + carry_ref[0:1, :]
    t = c.shape[0]
    carry_ref[...] = jnp.broadcast_to(c[t - 1:t, :], carry_ref.shape)
    cl = c * LOG2E
    ccol_ref[0] = cl
    ct = cl.T
    row = lax.broadcasted_iota(jnp.int32, (8, t), 0)
    ones_rows = jnp.where(row < 3, 1.0, 0.0)
    pad_rows = jnp.zeros((HEAD_DIM - 16, t), _F32)
    for h in range(FOX_HEADS):
        c1, c2, c3 = _split3(ct[h:h + 1, :])
        ck_rows = jnp.where(row == 0, -c1, jnp.where(row == 1, -c2, jnp.where(row == 2, -c3, 0.0)))
        bias_half = jnp.concatenate([ones_rows, ck_rows, pad_rows], axis=0)
        kh = kt[h * HEAD_DIM:(h + 1) * HEAD_DIM, :]
        halves = [kh, bias_half] if h % 2 == 0 else [bias_half, kh]
        kaug_ref[0, h, 0] = jnp.concatenate(halves, axis=0).astype(_BF16)


def _in_proj(x, w_att, bf_pad, tri):
    b, s, d = x.shape
    t = SEQ_TILE
    wn = w_att.shape[1]
    return pl.pallas_call(
        _in_proj_kernel,
        grid=(b, s // t),
        in_specs=[
            pl.BlockSpec((1, t, d), lambda i, j: (i, j, 0)),
            pl.BlockSpec((d, wn), lambda i, j: (0, 0)),
            pl.BlockSpec((1, LANES), lambda i, j: (0, 0)),
            pl.BlockSpec((t, t), lambda i, j: (0, 0)),
        ],
        out_specs=[
            pl.BlockSpec((1, t, ATT_W), lambda i, j: (i, j, 0)),
            pl.BlockSpec((1, t, LANES), lambda i, j: (i, j, 0)),
            pl.BlockSpec((1, FOX_HEADS, 1, LANES, t), lambda i, j: (i, 0, j, 0, 0)),
        ],
        out_shape=[
            jax.ShapeDtypeStruct((b, s, ATT_W), _BF16),
            jax.ShapeDtypeStruct((b, s, LANES), _F32),
            jax.ShapeDtypeStruct((b, FOX_HEADS, s // t, LANES, t), _BF16),
        ],
        scratch_shapes=[pltpu.VMEM((8, LANES), _F32)],
        compiler_params=_params("arbitrary", "arbitrary"),
        name="in_proj",
    )(x, w_att, bf_pad, tri)


def _alibi_slopes(n):
    return [float(v) for v in np.asarray(2.0 ** (-8.0 * np.arange(1, n + 1) / n), dtype=np.float32)]


def _swa_kernel(sink_ref, q_ref, kp_ref, kc_ref, vp_ref, vc_ref, o_ref):
    blk = pl.program_id(1)
    w = WINDOW
    q = q_ref[0] * jnp.asarray(HEAD_DIM ** -0.5, _BF16)
    k = jnp.concatenate([kp_ref[0], kc_ref[0]], axis=0)
    v = jnp.concatenate([vp_ref[0], vc_ref[0]], axis=0)
    row = lax.broadcasted_iota(jnp.int32, (w, 2 * w), 0)
    col = lax.broadcasted_iota(jnp.int32, (w, 2 * w), 1)
    dist = row + w - col
    valid = (dist >= 0) & (dist < w) & ((col >= w) | (blk > 0))
    distf = dist.astype(_F32)
    slopes = _alibi_slopes(SWA_Q_HEADS)
    outs = []
    for h in range(SWA_Q_HEADS):
        g = h // SWA_GROUP
        qh = q[:, h * HEAD_DIM:(h + 1) * HEAD_DIM]
        kg = k[:, g * HEAD_DIM:(g + 1) * HEAD_DIM]
        vg = v[:, g * HEAD_DIM:(g + 1) * HEAD_DIM]
        sc = _dot_nt(qh, kg) - slopes[h] * distf
        sc = jnp.where(valid, sc, NEG_INF)
        sk = sink_ref[h]
        m = jnp.maximum(jnp.max(sc, axis=1, keepdims=True), sk)
        p = jnp.exp(sc - m)
        denom = jnp.sum(p, axis=1, keepdims=True) + jnp.exp(sk - m)
        o = _dot(p.astype(_BF16), vg)
        outs.append(o / denom)
    o_ref[0] = jnp.concatenate(outs, axis=1).astype(_BF16)


def _swa(qkv, sink_flat):
    b, s, _ = qkv.shape
    w = WINDOW
    kcol = SWA_Q_W // LANES
    vcol = kcol + SWA_KV_W // LANES
    grid_spec = pltpu.PrefetchScalarGridSpec(
        num_scalar_prefetch=1,
        grid=(b, s // w),
        in_specs=[
            pl.BlockSpec((1, w, SWA_Q_W), lambda i, j, sk: (i, j, 0)),
            pl.BlockSpec((1, w, SWA_KV_W), lambda i, j, sk: (i, jnp.maximum(j - 1, 0), kcol)),
            pl.BlockSpec((1, w, SWA_KV_W), lambda i, j, sk: (i, j, kcol)),
            pl.BlockSpec((1, w, SWA_KV_W), lambda i, j, sk: (i, jnp.maximum(j - 1, 0), vcol)),
            pl.BlockSpec((1, w, SWA_KV_W), lambda i, j, sk: (i, j, vcol)),
        ],
        out_specs=pl.BlockSpec((1, w, SWA_Q_W), lambda i, j, sk: (i, j, 0)),
    )
    return pl.pallas_call(
        _swa_kernel,
        grid_spec=grid_spec,
        out_shape=jax.ShapeDtypeStruct((b, s, SWA_Q_W), _BF16),
        compiler_params=_params("arbitrary", "arbitrary"),
        name="swa",
    )(sink_flat, qkv, qkv, qkv, qkv, qkv)


def _fox_kernel(q_ref, kaug_ref, v_ref, ccol_ref, o_ref, m_ref, acc_ref, s_ref, p_ref):
    pair = pl.program_id(1)
    qi = pl.program_id(2)
    t = q_ref.shape[1]
    lane = lax.broadcasted_iota(jnp.int32, (t, LANES), 1)
    low = lane < HEAD_DIM
    q = q_ref[0].astype(_F32)
    ccol = ccol_ref[0]
    qa = []
    for hh in range(2):
        cq = jnp.sum(jnp.where(lane == 2 * pair + hh, ccol, 0.0), axis=1, keepdims=True)
        a1, a2, a3 = _split3(cq)
        base = HEAD_DIM if hh == 0 else 0
        bias = jnp.where(lane == base, a1, jnp.where(lane == base + 1, a2, jnp.where(lane == base + 2, a3, 0.0)))
        bias = jnp.where((lane >= base + 8) & (lane < base + 11), 1.0, bias)
        own = low if hh == 0 else ~low
        qa.append(jnp.where(own, q, bias).astype(_BF16))

    m_ref[...] = jnp.full_like(m_ref, NEG_INF)
    acc_ref[...] = jnp.zeros_like(acc_ref)
    rc = FOX_ROW_CHUNK
    lane_c = lax.broadcasted_iota(jnp.int32, (rc, LANES), 1)
    low_c = lane_c < HEAD_DIM
    ones0 = jnp.where(lane == 0, 1.0, 0.0).astype(_BF16)
    ones1 = jnp.where(lane == 1, 1.0, 0.0).astype(_BF16)

    def logits(slot, j):
        for hh in range(2):
            s_ref[slot, hh] = _dot(qa[hh], kaug_ref[0, hh, j])

    def absorb(slot, j, masked):
        start = pl.multiple_of(j * t, t)
        v = v_ref[0, pl.ds(start, t), :]
        zero = jnp.zeros_like(v)
        vcat = jnp.concatenate([jnp.concatenate([jnp.where(low, v, zero), ones0], axis=1),
                                jnp.concatenate([jnp.where(low, zero, v), ones1], axis=1)], axis=0)
        for r in range(t // rc):
            rows = pl.ds(r * rc, rc)
            if masked:
                keep = (lax.broadcasted_iota(jnp.int32, (rc, t), 1)
                        <= lax.broadcasted_iota(jnp.int32, (rc, t), 0) + r * rc)
            alphas = []
            for hh in range(2):
                s = s_ref[slot, hh, rows, :]
                if masked:
                    s = jnp.where(keep, s, NEG_INF)
                m_old = m_ref[hh, rows, :]
                m_new = jnp.maximum(m_old, jnp.max(s, axis=1, keepdims=True))
                alpha = jnp.exp2(m_old - m_new)
                p = jnp.exp2(s - jnp.tile(m_new, (1, t // LANES)))
                m_ref[hh, rows, :] = m_new
                p_ref[rows, hh * t:(hh + 1) * t] = p.astype(_BF16)
                alphas.append(alpha)
            scale = jnp.concatenate([jnp.where(low_c, alphas[0], alphas[1]),
                                     jnp.where(lane_c == 0, alphas[0], alphas[1])], axis=1)
            acc_ref[rows, :] = scale * acc_ref[rows, :]
        acc_ref[...] = acc_ref[...] + _dot(p_ref[...], vcat)

    logits(0, 0)

    def body(i, carry):
        logits(1, 2 * i + 1)
        absorb(0, 2 * i, False)
        logits(0, 2 * i + 2)
        absorb(1, 2 * i + 1, False)
        return carry

    lax.fori_loop(0, qi // 2, body, 0)

    @pl.when(qi % 2 == 0)
    def _():
        absorb(0, qi, True)

    @pl.when(qi % 2 == 1)
    def _():
        logits(1, qi)
        absorb(0, qi - 1, False)
        absorb(1, qi, True)

    acc = acc_ref[...]
    denom = jnp.where(low, acc[:, LANES:LANES + 1], acc[:, LANES + 1:LANES + 2])
    o_ref[0] = (acc[:, :LANES] / denom).astype(_BF16)


def _fox(qkv, ccol, kaug):
    b, s, _ = qkv.shape
    t = SEQ_TILE
    pairs = FOX_HEADS // 2
    qcol = (SWA_Q_W + 2 * SWA_KV_W) // LANES
    vcol = qcol + 2 * FOX_W // LANES
    return pl.pallas_call(
        _fox_kernel,
        grid=(b, pairs, s // t),
        in_specs=[
            pl.BlockSpec((1, t, LANES), lambda i, p, j: (i, j, qcol + p)),
            pl.BlockSpec((1, 2, s // t, LANES, t), lambda i, p, j: (i, p, 0, 0, 0)),
            pl.BlockSpec((1, s, LANES), lambda i, p, j: (i, 0, vcol + p)),
            pl.BlockSpec((1, t, LANES), lambda i, p, j: (i, j, 0)),
        ],
        out_specs=pl.BlockSpec((1, t, LANES), lambda i, p, j: (i, j, p)),
        out_shape=jax.ShapeDtypeStruct((b, s, FOX_W), _BF16),
        scratch_shapes=[
            pltpu.VMEM((2, t, LANES), _F32),
            pltpu.VMEM((t, 2 * LANES), _F32),
            pltpu.VMEM((2, 2, t, t), _F32),
            pltpu.VMEM((t, 2 * t), _BF16),
        ],
        compiler_params=_params("arbitrary", "arbitrary", "arbitrary"),
        name="fox",
    )(qkv, kaug, qkv, ccol)


def _mix_kernel(alpha, x_ref, aa_ref, af_ref, wg_ref, wps_ref, wpf_ref, wo_ref, g1_ref, b1_ref,
                wr_ref, br_ref, h_ref, hs_ref, idx_ref, gate_ref):
    d = x_ref.shape[1]
    t = x_ref.shape[0]
    x = x_ref[...]
    gates = _dot(x.astype(_BF16), wg_ref[...])
    ya = _dot(aa_ref[...], wps_ref[...])
    yf = _dot(af_ref[...], wpf_ref[...])
    mix = jax.nn.sigmoid(gates[:, :d]) * ya + jax.nn.sigmoid(gates[:, d:]) * yf
    z = alpha * x + _dot(mix.astype(_BF16), wo_ref[...])
    h = _layer_norm(z, g1_ref[...], b1_ref[...])
    h_ref[...] = h
    _store_slabs(hs_ref, h)

    logits = _dot(h.astype(_BF16), wr_ref[...]) + br_ref[...]
    lt = logits.T[:N_EXPERTS, :]
    eidx = lax.broadcasted_iota(jnp.int32, (N_EXPERTS, t), 0)
    work = lt
    vals, idxs = [], []
    for _ in range(TOP_K):
        mk = jnp.max(work, axis=0, keepdims=True)
        ik = jnp.min(jnp.where(work == mk, eidx, N_EXPERTS), axis=0, keepdims=True)
        vals.append(mk)
        idxs.append(ik)
        work = jnp.where(eidx == ik, NEG_INF, work)
    ex = [jnp.exp(vk - vals[0]) for vk in vals]
    denom = ex[0] + ex[1] + ex[2] + ex[3]
    idx_ref[...] = jnp.concatenate(idxs + [jnp.zeros((8 - TOP_K, t), jnp.int32)], axis=0)
    gt = jnp.concatenate([e / denom for e in ex] + [jnp.zeros((LANES - TOP_K, t), _F32)], axis=0)
    gate_ref[...] = gt.T


def _mix(alpha, x2, att_a, att_f, wg, wps, wpf, wo, g1, b1, wr, br):
    n, d = x2.shape
    t = SEQ_TILE
    const = lambda shape: pl.BlockSpec(shape, lambda i: (0, 0))
    return pl.pallas_call(
        functools.partial(_mix_kernel, alpha),
        grid=(n // t,),
        in_specs=[
            pl.BlockSpec((t, d), lambda i: (i, 0)),
            pl.BlockSpec((t, SWA_Q_W), lambda i: (i, 0)),
            pl.BlockSpec((t, FOX_W), lambda i: (i, 0)),
            const(wg.shape), const(wps.shape), const(wpf.shape), const(wo.shape),
            const(g1.shape), const(b1.shape), const(wr.shape), const(br.shape),
        ],
        out_specs=[
            pl.BlockSpec((t, d), lambda i: (i, 0)),
            pl.BlockSpec((t * d // LANES, LANES), lambda i: (i, 0)),
            pl.BlockSpec((8, t), lambda i: (0, i)),
            pl.BlockSpec((t, LANES), lambda i: (i, 0)),
        ],
        out_shape=[
            jax.ShapeDtypeStruct((n, d), _F32),
            jax.ShapeDtypeStruct((n * d // LANES, LANES), _F32),
            jax.ShapeDtypeStruct((8, n), jnp.int32),
            jax.ShapeDtypeStruct((n, LANES), _F32),
        ],
        compiler_params=_params("arbitrary"),
        name="mix",
    )(x2, att_a, att_f, wg, wps, wpf, wo, g1, b1, wr, br)


def _route_kernel(idx_ref, tri_ref, slot_ref, meta_ref, cnt_ref, run_ref):
    phase = pl.program_id(0)
    i = pl.program_id(1)
    t = idx_ref.shape[1]
    blk = EXPERT_ROWS
    eidx = lax.broadcasted_iota(jnp.int32, (N_EXPERTS, t), 0)
    idx = idx_ref[...]
    sel = (eidx == idx[0:1, :])
    for k in range(1, TOP_K):
        sel = sel | (eidx == idx[k:k + 1, :])
    self32 = jnp.where(sel, 1.0, 0.0)
    tile_cnt = jnp.sum(self32, axis=1, keepdims=True)

    @pl.when((phase == 0) & (i == 0))
    def _():
        cnt_ref[...] = jnp.zeros_like(cnt_ref)

    @pl.when(phase == 0)
    def _():
        cnt_ref[...] = cnt_ref[...] + tile_cnt

    @pl.when((phase == 1) & (i == 0))
    def _():
        cnt = cnt_ref[...]
        padded = jnp.floor((cnt + (blk - 1)) * (1.0 / blk)) * blk
        run = jnp.zeros((1, LANES), _F32)
        rows = []
        for e in range(N_EXPERTS):
            rows.append(run)
            run = run + padded[e:e + 1, :]
        start = jnp.concatenate(rows, axis=0)
        run_ref[...] = start
        nbp = meta_ref.shape[1]
        s1 = start[:, 0:1]
        e1 = s1 + padded[:, 0:1]
        c1 = cnt[:, 0:1]
        row0 = (lax.broadcasted_iota(jnp.int32, (N_EXPERTS, nbp), 1) * blk).astype(_F32)
        owner = jnp.sum(jnp.where(e1 <= row0, 1.0, 0.0), axis=0, keepdims=True)
        owner = jnp.minimum(owner, N_EXPERTS - 1.0)
        inside = (s1 <= row0) & (row0 < e1)
        nval = jnp.sum(jnp.where(inside, jnp.clip(c1 - (row0 - s1), 0.0, blk), 0.0), axis=0, keepdims=True)
        erow = lax.broadcasted_iota(jnp.int32, (N_EXPERTS, nbp), 0).astype(_F32)
        later = (erow > owner) & (c1 > 0.0)
        nxt = jnp.min(jnp.where(later, erow, float(N_EXPERTS)), axis=0, keepdims=True)
        meta_ref[...] = jnp.concatenate(
            [owner.astype(jnp.int32), nval.astype(jnp.int32), nxt.astype(jnp.int32),
             jnp.zeros((5, nbp), jnp.int32)], axis=0)

    @pl.when(phase == 1)
    def _():
        rank = _dot(self32.astype(_BF16), tri_ref[...])
        slot = (run_ref[:, 0:1] + rank).astype(jnp.int32)
        for k in range(TOP_K):
            sk = jnp.sum(jnp.where(eidx == idx[k:k + 1, :], slot, 0), axis=0, keepdims=True)
            slot_ref[0, :, k * t:(k + 1) * t] = sk
        run_ref[...] = run_ref[...] + tile_cnt


def _route(idx_t, tri_strict, n_blocks_pad):
    n = idx_t.shape[1]
    t = SEQ_TILE
    nt = n // t
    return pl.pallas_call(
        _route_kernel,
        grid=(2, nt),
        in_specs=[
            pl.BlockSpec((8, t), lambda ph, i: (0, i)),
            pl.BlockSpec((t, t), lambda ph, i: (0, 0)),
        ],
        out_specs=[
            pl.BlockSpec((1, 1, TOP_K * t), lambda ph, i: (ph * i, 0, 0)),
            pl.BlockSpec((8, n_blocks_pad), lambda ph, i: (0, 0)),
        ],
        out_shape=[
            jax.ShapeDtypeStruct((nt, 1, TOP_K * t), jnp.int32),
            jax.ShapeDtypeStruct((8, n_blocks_pad), jnp.int32),
        ],
        scratch_shapes=[
            pltpu.VMEM((N_EXPERTS, LANES), _F32),
            pltpu.VMEM((N_EXPERTS, LANES), _F32),
        ],
        compiler_params=_params("arbitrary", "arbitrary"),
        name="route",
    )(idx_t, tri_strict)


SLAB = 8
DRAIN_UNROLL = 8


def _row_copy(src_ref, src_row, dst_ref, dst_row, sem):
    src = src_ref.at[pl.ds(pl.multiple_of(src_row * SLAB, SLAB), SLAB), :]
    dst = dst_ref.at[pl.ds(pl.multiple_of(dst_row * SLAB, SLAB), SLAB), :]
    return pltpu.make_async_copy(src, dst, sem)


def _dispatch_kernel(meta_ref, slot_ref, h_ref, xs_ref, zero_buf, sem, zsem):
    t = h_ref.shape[0] // SLAB
    blk = zero_buf.shape[0]

    @pl.when(pl.program_id(0) == 0)
    def _():
        zero_buf[...] = jnp.zeros_like(zero_buf)
        n_blocks = xs_ref.shape[0] // blk

        def fill(bi):
            return pltpu.make_async_copy(zero_buf, xs_ref.at[pl.ds(pl.multiple_of(bi * blk, blk), blk), :], zsem)

        def start(bi, carry):
            @pl.when(meta_ref[1, bi] < EXPERT_ROWS)
            def _():
                fill(bi).start()
            return carry

        def wait(bi, carry):
            @pl.when(meta_ref[1, bi] < EXPERT_ROWS)
            def _():
                fill(bi).wait()
            return carry

        lax.fori_loop(0, n_blocks, start, 0)
        lax.fori_loop(0, n_blocks, wait, 0)

    def issue(r, carry):
        for k in range(TOP_K):
            _row_copy(h_ref, r, xs_ref, slot_ref[0, 0, k * t + r], sem).start()
        return carry

    lax.fori_loop(0, t, issue, 0)

    def drain(r8, carry):
        for u in range(DRAIN_UNROLL):
            for k in range(TOP_K):
                _row_copy(h_ref, r8 * DRAIN_UNROLL + u, xs_ref, 0, sem).wait()
        return carry

    lax.fori_loop(0, t // DRAIN_UNROLL, drain, 0)


def _dispatch(meta, slots, hs, n_rows):
    n = hs.shape[0] // SLAB
    t = SEQ_TILE
    grid_spec = pltpu.PrefetchScalarGridSpec(
        num_scalar_prefetch=1,
        grid=(n // t,),
        in_specs=[
            pl.BlockSpec((1, 1, TOP_K * t), lambda i, m: (i, 0, 0), memory_space=pltpu.SMEM),
            pl.BlockSpec((t * SLAB, LANES), lambda i, m: (i, 0)),
        ],
        out_specs=pl.BlockSpec(memory_space=pl.ANY),
        scratch_shapes=[pltpu.VMEM((EXPERT_ROWS * SLAB, LANES), _F32), pltpu.SemaphoreType.DMA(()),
                        pltpu.SemaphoreType.DMA(())],
    )
    return pl.pallas_call(
        _dispatch_kernel,
        grid_spec=grid_spec,
        out_shape=jax.ShapeDtypeStruct((n_rows * SLAB, LANES), _F32),
        compiler_params=_params("arbitrary"),
        name="dispatch",
    )(meta, slots, hs)


def _expert_kernel(meta_ref, x_ref, wgu_hbm, bgu_ref, wd_hbm, bd_ref, y_ref,
                   wgu_land, wd_land, wgu_bf, wd_bf, sems, turn_ref):
    b = pl.program_id(0)
    e = meta_ref[0, b]
    nval = meta_ref[1, b]
    prev = meta_ref[0, jnp.maximum(b - 1, 0)]
    f = wd_bf.shape[0]

    def fetch(expert, slot):
        return (pltpu.make_async_copy(wgu_hbm.at[expert], wgu_land.at[slot], sems.at[0, slot]),
                pltpu.make_async_copy(wd_hbm.at[expert], wd_land.at[slot], sems.at[1, slot]))

    @pl.when(b == 0)
    def _():
        turn_ref[0] = 0
        for cp in fetch(e, 0):
            cp.start()

    @pl.when((nval > 0) & ((b == 0) | (prev != e)))
    def _():
        slot = turn_ref[0]
        for cp in fetch(e, slot):
            cp.wait()
        nxt = meta_ref[2, b]

        @pl.when(nxt < N_EXPERTS)
        def _():
            for cp in fetch(nxt, 1 - slot):
                cp.start(priority=1)

        wgu_bf[...] = wgu_land[slot].astype(_BF16)
        wd_bf[...] = wd_land[slot].astype(_BF16)
        turn_ref[0] = 1 - slot

    @pl.when(nval > 0)
    def _():
        x = _load_slabs(x_ref, EXPERT_ROWS, SLAB).astype(_BF16)
        hgu = _dot(x, wgu_bf[...]) + bgu_ref[0]
        gate = jnp.minimum(hgu[:, :f], SWIGLU_LIMIT)
        up = jnp.clip(hgu[:, f:], -SWIGLU_LIMIT, SWIGLU_LIMIT)
        act = gate * jax.nn.sigmoid(gate * SWIGLU_ALPHA) * (up + 1.0)
        _store_slabs(y_ref, _dot(act.astype(_BF16), wd_bf[...]) + bd_ref[0])

    @pl.when(nval <= 0)
    def _():
        y_ref[...] = jnp.zeros_like(y_ref)


def _experts(meta, xs, w_gu, b_gu, w_d, b_d):
    ne, d, f2 = w_gu.shape
    assert d == SLAB * LANES
    r = xs.shape[0] // SLAB
    f = w_d.shape[1]
    blk = EXPERT_ROWS
    grid_spec = pltpu.PrefetchScalarGridSpec(
        num_scalar_prefetch=1,
        grid=(r // blk,),
        in_specs=[
            pl.BlockSpec((blk * SLAB, LANES), lambda i, m: (i, 0)),
            pl.BlockSpec(memory_space=pl.ANY),
            pl.BlockSpec((1, 1, f2), lambda i, m: (m[0, i], 0, 0)),
            pl.BlockSpec(memory_space=pl.ANY),
            pl.BlockSpec((1, 1, d), lambda i, m: (m[0, i], 0, 0)),
        ],
        out_specs=pl.BlockSpec((blk * SLAB, LANES), lambda i, m: (i, 0)),
        scratch_shapes=[
            pltpu.VMEM((2, d, f2), _F32), pltpu.VMEM((2, f, d), _F32),
            pltpu.VMEM((d, f2), _BF16), pltpu.VMEM((f, d), _BF16),
            pltpu.SemaphoreType.DMA((2, 2)), pltpu.SMEM((1,), jnp.int32),
        ],
    )
    return pl.pallas_call(
        _expert_kernel,
        grid_spec=grid_spec,
        out_shape=jax.ShapeDtypeStruct((r * SLAB, LANES), _F32),
        compiler_params=_params("arbitrary"),
        name="experts",
    )(meta, xs, w_gu, b_gu.reshape(ne, 1, f2), w_d, b_d.reshape(ne, 1, d))


def _combine_kernel(alpha, slot_ref, h_ref, gate_ref, g2_ref, b2_ref, ys_ref, o_ref, buf, sem):
    t = h_ref.shape[0]

    def issue(r, carry):
        for k in range(TOP_K):
            _row_copy(ys_ref, slot_ref[0, 0, k * t + r], buf.at[k], r, sem).start()
        return carry

    lax.fori_loop(0, t, issue, 0)

    def drain(r8, carry):
        for u in range(DRAIN_UNROLL):
            for k in range(TOP_K):
                _row_copy(ys_ref, 0, buf.at[k], r8 * DRAIN_UNROLL + u, sem).wait()
        return carry

    lax.fori_loop(0, t // DRAIN_UNROLL, drain, 0)

    gate = gate_ref[...]
    moe = gate[:, 0:1] * _load_slabs(buf, t, SLAB, (0,))
    for k in range(1, TOP_K):
        moe = moe + gate[:, k:k + 1] * _load_slabs(buf, t, SLAB, (k,))
    o_ref[...] = _layer_norm(alpha * h_ref[...] + moe, g2_ref[...], b2_ref[...])


def _combine(alpha, slots, h, gate, g2, b2, ys):
    n, d = h.shape
    t = SEQ_TILE
    return pl.pallas_call(
        functools.partial(_combine_kernel, alpha),
        grid=(n // t,),
        in_specs=[
            pl.BlockSpec((1, 1, TOP_K * t), lambda i: (i, 0, 0), memory_space=pltpu.SMEM),
            pl.BlockSpec((t, d), lambda i: (i, 0)),
            pl.BlockSpec((t, LANES), lambda i: (i, 0)),
            pl.BlockSpec((1, d), lambda i: (0, 0)),
            pl.BlockSpec((1, d), lambda i: (0, 0)),
            pl.BlockSpec(memory_space=pl.ANY),
        ],
        out_specs=pl.BlockSpec((t, d), lambda i: (i, 0)),
        out_shape=jax.ShapeDtypeStruct((n, d), _F32),
        scratch_shapes=[pltpu.VMEM((TOP_K, t * SLAB, LANES), _F32), pltpu.SemaphoreType.DMA(())],
        compiler_params=_params("arbitrary"),
        name="combine",
    )(slots, h, gate, g2, b2, ys)


def _pad_cols(a, width):
    return jnp.pad(a, ((0, 0), (0, width - a.shape[1])))


def kernel(x, w_in, b_forget, sink, w_proj_swa, w_proj_fox, w_out, ln1_g, ln1_b, w_router, b_router,
           w_gate_up, b_gate_up, w_down, b_down, ln2_g, ln2_b):
    b, s, d = x.shape
    depth = w_in.shape[0]
    n = b * s
    alpha = float((2.0 * depth) ** 0.25)
    assert s % SEQ_TILE == 0 and SEQ_TILE % WINDOW == 0 and d % LANES == 0
    assert w_router.shape[2] == N_EXPERTS and w_in.shape[2] == ATT_W + FOX_HEADS + 2 * d

    blk = EXPERT_ROWS
    n_rows = ((n * TOP_K + blk - 1) // blk) * blk + N_EXPERTS * blk
    n_blocks_pad = ((n_rows // blk + LANES - 1) // LANES) * LANES
    t = SEQ_TILE
    ri = lax.broadcasted_iota(jnp.int32, (t, t), 0)
    ci = lax.broadcasted_iota(jnp.int32, (t, t), 1)
    tri_incl = (ci <= ri).astype(_BF16)
    tri_strict = (ri < ci).astype(_BF16)

    h = x
    for layer in range(depth):
        w_l = w_in[layer]
        w_att = _pad_cols(w_l[:, :ATT_W + FOX_HEADS], ATT_W + LANES).astype(_BF16)
        w_gates = w_l[:, ATT_W + FOX_HEADS:].astype(_BF16)
        bf_pad = _pad_cols(b_forget[layer].reshape(1, FOX_HEADS).astype(_F32), LANES)

        qkv, ccol, kaug = _in_proj(h, w_att, bf_pad, tri_incl)
        att_a = _swa(qkv, sink[layer].reshape(-1).astype(_F32))
        att_f = _fox(qkv, ccol, kaug)

        h1, h1_slabs, idx_t, gate = _mix(
            alpha, h.reshape(n, d), att_a.reshape(n, SWA_Q_W), att_f.reshape(n, FOX_W),
            w_gates, w_proj_swa[layer].astype(_BF16), w_proj_fox[layer].astype(_BF16),
            w_out[layer].astype(_BF16), ln1_g[layer].reshape(1, d), ln1_b[layer].reshape(1, d),
            _pad_cols(w_router[layer], LANES).astype(_BF16),
            _pad_cols(b_router[layer].reshape(1, N_EXPERTS).astype(_F32), LANES))

        slots, meta = _route(idx_t, tri_strict, n_blocks_pad)
        xs = _dispatch(meta, slots, h1_slabs, n_rows)
        ys = _experts(meta, xs, w_gate_up[layer], b_gate_up[layer], w_down[layer], b_down[layer])
        out = _combine(alpha, slots, h1, gate, ln2_g[layer].reshape(1, d), ln2_b[layer].reshape(1, d), ys)
        h = out.reshape(b, s, d)
    return h
```

```python
import functools

import numpy as np
import jax
import jax.numpy as jnp
from jax import lax
from jax.experimental import pallas as pl
from jax.experimental.pallas import tpu as pltpu

HEAD_DIM = 64
SWA_Q_HEADS = 8
SWA_KV_HEADS = 2
SWA_GROUP = SWA_Q_HEADS // SWA_KV_HEADS
WINDOW = 128
FOX_HEADS = 8
N_EXPERTS = 32
TOP_K = 4
SWIGLU_LIMIT = 7.0
SWIGLU_ALPHA = 1.702
LN_EPS = 1e-5

LANES = 128
SEQ_TILE = 512
EXPERT_ROWS = 256
MIX_SPLIT = 2
FOX_ROW_CHUNK = 64
VMEM_LIMIT = 56 * 1024 * 1024

SWA_Q_W = SWA_Q_HEADS * HEAD_DIM
SWA_KV_W = SWA_KV_HEADS * HEAD_DIM
FOX_W = FOX_HEADS * HEAD_DIM
ATT_W = SWA_Q_W + 2 * SWA_KV_W + 3 * FOX_W
NEG_INF = float("-inf")
LOG2E = 1.4426950408889634
FOX_Q_SCALE = HEAD_DIM ** -0.5 * LOG2E

_F32 = jnp.float32
_BF16 = jnp.bfloat16


def _dot(a, b):
    return jnp.dot(a, b, preferred_element_type=_F32)


def _dot_nt(a, b):
    return lax.dot_general(a, b, (((1,), (1,)), ((), ())), preferred_element_type=_F32)


def _params(*sem):
    return pltpu.CompilerParams(dimension_semantics=sem, vmem_limit_bytes=VMEM_LIMIT)


def _layer_norm(z, g, b):
    mu = jnp.mean(z, axis=-1, keepdims=True)
    zc = z - mu
    var = jnp.mean(zc * zc, axis=-1, keepdims=True)
    return zc * lax.rsqrt(var + LN_EPS) * g + b


def _store_slabs(ref, value, lead=()):
    rows, width = value.shape
    c = width // LANES
    for j in range(c):
        ref[lead + (pl.ds(j, rows, stride=c), slice(None))] = value[:, j * LANES:(j + 1) * LANES]


def _load_slabs(ref, rows, c, lead=()):
    return jnp.concatenate([ref[lead + (pl.ds(j, rows, stride=c), slice(None))] for j in range(c)], axis=1)


def _split3(v):
    p1 = v.astype(_BF16).astype(_F32)
    r1 = v - p1
    p2 = r1.astype(_BF16).astype(_F32)
    p3 = (r1 - p2).astype(_BF16).astype(_F32)
    return p1, p2, p3


def _in_proj_kernel(x_ref, w_ref, bf_ref, tri_ref, qkv_ref, ccol_ref, kaug_ref, carry_ref):
    @pl.when(pl.program_id(1) == 0)
    def _():
        carry_ref[...] = jnp.zeros_like(carry_ref)

    xb = x_ref[0].astype(_BF16)
    acc = _dot(xb, w_ref[...])
    qf_col = SWA_Q_W + 2 * SWA_KV_W
    qkv_ref[0, :, :qf_col] = acc[:, :qf_col].astype(_BF16)
    qkv_ref[0, :, qf_col:qf_col + FOX_W] = (acc[:, qf_col:qf_col + FOX_W] * FOX_Q_SCALE).astype(_BF16)
    qkv_ref[0, :, qf_col + FOX_W:] = acc[:, qf_col + FOX_W:ATT_W].astype(_BF16)
    kf_col = SWA_Q_W + 2 * SWA_KV_W + FOX_W
    kt = acc[:, kf_col:kf_col + FOX_W].T
    z = acc[:, ATT_W:] + bf_ref[...]
    log_f = jnp.minimum(z, 0.0) - jnp.log1p(jnp.exp(-jnp.abs(z)))
    tri = tri_ref[...]
    p1 = log_f.astype(_BF16)
    r1 = log_f - p1.astype(_F32)
    p2 = r1.astype(_BF16)
    p3 = (r1 - p2.astype(_F32)).astype(_BF16)
    c = _dot(tri, p1) + _dot(tri, p2) + _dot(tri, p3) + carry_ref[0:1, :]
    t = c.shape[0]
    carry_ref[...] = jnp.broadcast_to(c[t - 1:t, :], carry_ref.shape)
    cl = c * LOG2E
    ccol_ref[0] = cl
    ct = cl.T
    row = lax.broadcasted_iota(jnp.int32, (8, t), 0)
    ones_rows = jnp.where(row == 0, 1.0, 0.0)
    pad_rows = jnp.zeros((HEAD_DIM - 16, t), _F32)
    for h in range(FOX_HEADS):
        c1, c2, c3 = _split3(ct[h:h + 1, :])
        ck_rows = jnp.where(row == 0, -c1, jnp.where(row == 1, -c2, jnp.where(row == 2, -c3, 0.0)))
        bias_half = jnp.concatenate([ones_rows, ck_rows, pad_rows], axis=0)
        kh = kt[h * HEAD_DIM:(h + 1) * HEAD_DIM, :]
        halves = [kh, bias_half] if h % 2 == 0 else [bias_half, kh]
        kaug_ref[0, h, 0] = jnp.concatenate(halves, axis=0).astype(_BF16)


def _in_proj(x, w_att, bf_pad, tri):
    b, s, d = x.shape
    t = SEQ_TILE
    wn = w_att.shape[1]
    return pl.pallas_call(
        _in_proj_kernel,
        grid=(b, s // t),
        in_specs=[
            pl.BlockSpec((1, t, d), lambda i, j: (i, j, 0)),
            pl.BlockSpec((d, wn), lambda i, j: (0, 0)),
            pl.BlockSpec((1, LANES), lambda i, j: (0, 0)),
            pl.BlockSpec((t, t), lambda i, j: (0, 0)),
        ],
        out_specs=[
            pl.BlockSpec((1, t, ATT_W), lambda i, j: (i, j, 0)),
            pl.BlockSpec((1, t, LANES), lambda i, j: (i, j, 0)),
            pl.BlockSpec((1, FOX_HEADS, 1, LANES, t), lambda i, j: (i, 0, j, 0, 0)),
        ],
        out_shape=[
            jax.ShapeDtypeStruct((b, s, ATT_W), _BF16),
            jax.ShapeDtypeStruct((b, s, LANES), _F32),
            jax.ShapeDtypeStruct((b, FOX_HEADS, s // t, LANES, t), _BF16),
        ],
        scratch_shapes=[pltpu.VMEM((8, LANES), _F32)],
        compiler_params=_params("arbitrary", "arbitrary"),
        name="in_proj",
    )(x, w_att, bf_pad, tri)


def _alibi_slopes(n):
    return [float(v) for v in np.asarray(2.0 ** (-8.0 * np.arange(1, n + 1) / n), dtype=np.float32)]


def _swa_kernel(sink_ref, q_ref, kp_ref, kc_ref, vp_ref, vc_ref, o_ref):
    blk = pl.program_id(1)
    w = WINDOW
    q = q_ref[0] * jnp.asarray(HEAD_DIM ** -0.5, _BF16)
    k = jnp.concatenate([kp_ref[0], kc_ref[0]], axis=0)
    v = jnp.concatenate([vp_ref[0], vc_ref[0]], axis=0)
    row = lax.broadcasted_iota(jnp.int32, (w, 2 * w), 0)
    col = lax.broadcasted_iota(jnp.int32, (w, 2 * w), 1)
    dist = row + w - col
    valid = (dist >= 0) & (dist < w) & ((col >= w) | (blk > 0))
    distf = dist.astype(_F32)
    slopes = _alibi_slopes(SWA_Q_HEADS)
    outs = []
    for h in range(SWA_Q_HEADS):
        g = h // SWA_GROUP
        qh = q[:, h * HEAD_DIM:(h + 1) * HEAD_DIM]
        kg = k[:, g * HEAD_DIM:(g + 1) * HEAD_DIM]
        vg = v[:, g * HEAD_DIM:(g + 1) * HEAD_DIM]
        sc = _dot_nt(qh, kg) - slopes[h] * distf
        sc = jnp.where(valid, sc, NEG_INF)
        sk = sink_ref[h]
        m = jnp.maximum(jnp.max(sc, axis=1, keepdims=True), sk)
        p = jnp.exp(sc - m)
        denom = jnp.sum(p, axis=1, keepdims=True) + jnp.exp(sk - m)
        o = _dot(p.astype(_BF16), vg)
        outs.append(o / denom)
    o_ref[0] = jnp.concatenate(outs, axis=1).astype(_BF16)


def _swa(qkv, sink_flat):
    b, s, _ = qkv.shape
    w = WINDOW
    kcol = SWA_Q_W // LANES
    vcol = kcol + SWA_KV_W // LANES
    grid_spec = pltpu.PrefetchScalarGridSpec(
        num_scalar_prefetch=1,
        grid=(b, s // w),
        in_specs=[
            pl.BlockSpec((1, w, SWA_Q_W), lambda i, j, sk: (i, j, 0)),
            pl.BlockSpec((1, w, SWA_KV_W), lambda i, j, sk: (i, jnp.maximum(j - 1, 0), kcol)),
            pl.BlockSpec((1, w, SWA_KV_W), lambda i, j, sk: (i, j, kcol)),
            pl.BlockSpec((1, w, SWA_KV_W), lambda i, j, sk: (i, jnp.maximum(j - 1, 0), vcol)),
            pl.BlockSpec((1, w, SWA_KV_W), lambda i, j, sk: (i, j, vcol)),
        ],
        out_specs=pl.BlockSpec((1, w, SWA_Q_W), lambda i, j, sk: (i, j, 0)),
    )
    return pl.pallas_call(
        _swa_kernel,
        grid_spec=grid_spec,
        out_shape=jax.ShapeDtypeStruct((b, s, SWA_Q_W), _BF16),
        compiler_params=_params("arbitrary", "arbitrary"),
        name="swa",
    )(sink_flat, qkv, qkv, qkv, qkv, qkv)


def _fox_kernel(q_ref, kaug_ref, v_ref, ccol_ref, o_ref, m_ref, acc_ref, s_ref, p_ref):
    pair = pl.program_id(1)
    qi = pl.program_id(2)
    t = q_ref.shape[1]
    lane = lax.broadcasted_iota(jnp.int32, (t, LANES), 1)
    low = lane < HEAD_DIM
    q = q_ref[0].astype(_F32)
    ccol = ccol_ref[0]
    qa = []
    for hh in range(2):
        cq = jnp.sum(jnp.where(lane == 2 * pair + hh, ccol, 0.0), axis=1, keepdims=True)
        base = HEAD_DIM if hh == 0 else 0
        bias = jnp.where(lane == base, cq, 0.0)
        bias = jnp.where((lane >= base + 8) & (lane < base + 11), 1.0, bias)
        own = low if hh == 0 else ~low
        qa.append(jnp.where(own, q, bias).astype(_BF16))

    m_ref[...] = jnp.full_like(m_ref, NEG_INF)
    acc_ref[...] = jnp.zeros_like(acc_ref)
    rc = FOX_ROW_CHUNK
    low_c = lax.broadcasted_iota(jnp.int32, (rc, LANES), 1) < HEAD_DIM
    ones0 = jnp.where(low, 1.0, 0.0).astype(_BF16)
    ones1 = jnp.where(low, 0.0, 1.0).astype(_BF16)

    def logits(slot, j):
        for hh in range(2):
            s_ref[slot, hh] = _dot(qa[hh], kaug_ref[0, hh, j])

    def absorb(slot, j, masked):
        start = pl.multiple_of(j * t, t)
        v = v_ref[0, pl.ds(start, t), :]
        zero = jnp.zeros_like(v)
        vcat = jnp.concatenate([jnp.concatenate([jnp.where(low, v, zero), ones0], axis=1),
                                jnp.concatenate([jnp.where(low, zero, v), ones1], axis=1)], axis=0)
        for r in range(t // rc):
            rows = pl.ds(r * rc, rc)
            if masked:
                keep = (lax.broadcasted_iota(jnp.int32, (rc, t), 1)
                        <= lax.broadcasted_iota(jnp.int32, (rc, t), 0) + r * rc)
            alphas = []
            for hh in range(2):
                s = s_ref[slot, hh, rows, :]
                if masked:
                    s = jnp.where(keep, s, NEG_INF)
                m_old = m_ref[hh, rows, :]
                m_new = jnp.maximum(m_old, jnp.max(s, axis=1, keepdims=True))
                alpha = jnp.exp2(m_old - m_new)
                p = jnp.exp2(s - jnp.tile(m_new, (1, t // LANES)))
                m_ref[hh, rows, :] = m_new
                p_ref[rows, hh * t:(hh + 1) * t] = p.astype(_BF16)
                alphas.append(alpha)
            scale = jnp.where(low_c, alphas[0], alphas[1])
            acc_ref[rows, :] = jnp.concatenate([scale, scale], axis=1) * acc_ref[rows, :]
        acc_ref[...] = acc_ref[...] + _dot(p_ref[...], vcat)

    logits(0, 0)

    def body(i, carry):
        logits(1, 2 * i + 1)
        absorb(0, 2 * i, False)
        logits(0, 2 * i + 2)
        absorb(1, 2 * i + 1, False)
        return carry

    lax.fori_loop(0, qi // 2, body, 0)

    @pl.when(qi % 2 == 0)
    def _():
        absorb(0, qi, True)

    @pl.when(qi % 2 == 1)
    def _():
        logits(1, qi)
        absorb(0, qi - 1, False)
        absorb(1, qi, True)

    acc = acc_ref[...]
    o_ref[0] = (acc[:, :LANES] / acc[:, LANES:]).astype(_BF16)


def _fox(qkv, ccol, kaug):
    b, s, _ = qkv.shape
    t = SEQ_TILE
    pairs = FOX_HEADS // 2
    qcol = (SWA_Q_W + 2 * SWA_KV_W) // LANES
    vcol = qcol + 2 * FOX_W // LANES
    return pl.pallas_call(
        _fox_kernel,
        grid=(b, pairs, s // t),
        in_specs=[
            pl.BlockSpec((1, t, LANES), lambda i, p, j: (i, j, qcol + p)),
            pl.BlockSpec((1, 2, s // t, LANES, t), lambda i, p, j: (i, p, 0, 0, 0)),
            pl.BlockSpec((1, s, LANES), lambda i, p, j: (i, 0, vcol + p)),
            pl.BlockSpec((1, t, LANES), lambda i, p, j: (i, j, 0)),
        ],
        out_specs=pl.BlockSpec((1, t, LANES), lambda i, p, j: (i, j, p)),
        out_shape=jax.ShapeDtypeStruct((b, s, FOX_W), _BF16),
        scratch_shapes=[
            pltpu.VMEM((2, t, LANES), _F32),
            pltpu.VMEM((t, 2 * LANES), _F32),
            pltpu.VMEM((2, 2, t, t), _F32),
            pltpu.VMEM((t, 2 * t), _BF16),
        ],
        compiler_params=_params("arbitrary", "arbitrary", "arbitrary"),
        name="fox",
    )(qkv, kaug, qkv, ccol)


def _mix_kernel(alpha, x_ref, aa_ref, af_ref, wg_ref, wps_ref, wpf_ref, wo_ref, g1_ref, b1_ref,
                wr_ref, br_ref, h_ref, hs_ref, idx_ref, gate_ref):
    d = x_ref.shape[1]
    t = x_ref.shape[0] // MIX_SPLIT
    for part in range(MIX_SPLIT):
        rows = slice(part * t, (part + 1) * t)
        x = x_ref[rows, :]
        gates = _dot(x.astype(_BF16), wg_ref[...])
        ya = _dot(aa_ref[rows, :], wps_ref[...])
        yf = _dot(af_ref[rows, :], wpf_ref[...])
        mix = jax.nn.sigmoid(gates[:, :d]) * ya + jax.nn.sigmoid(gates[:, d:]) * yf
        z = alpha * x + _dot(mix.astype(_BF16), wo_ref[...])
        h = _layer_norm(z, g1_ref[...], b1_ref[...])
        h_ref[rows, :] = h
        c = d // LANES
        for j in range(c):
            hs_ref[pl.ds(part * t * c + j, t, stride=c), :] = h[:, j * LANES:(j + 1) * LANES]

        logits = _dot(h.astype(_BF16), wr_ref[...]) + br_ref[...]
        lt = logits.T[:N_EXPERTS, :]
        eidx = lax.broadcasted_iota(jnp.int32, (N_EXPERTS, t), 0)
        work = lt
        vals, idxs = [], []
        for _ in range(TOP_K):
            mk = jnp.max(work, axis=0, keepdims=True)
            ik = jnp.min(jnp.where(work == mk, eidx, N_EXPERTS), axis=0, keepdims=True)
            vals.append(mk)
            idxs.append(ik)
            work = jnp.where(eidx == ik, NEG_INF, work)
        ex = [jnp.exp(vk - vals[0]) for vk in vals]
        denom = ex[0] + ex[1] + ex[2] + ex[3]
        idx_ref[:, rows] = jnp.concatenate(idxs + [jnp.zeros((8 - TOP_K, t), jnp.int32)], axis=0)
        gt = jnp.concatenate([e / denom for e in ex] + [jnp.zeros((LANES - TOP_K, t), _F32)], axis=0)
        gate_ref[rows, :] = gt.T


def _mix(alpha, x2, att_a, att_f, wg, wps, wpf, wo, g1, b1, wr, br):
    n, d = x2.shape
    t = SEQ_TILE
    const = lambda shape: pl.BlockSpec(shape, lambda i: (0, 0))
    return pl.pallas_call(
        functools.partial(_mix_kernel, alpha),
        grid=(n // t,),
        in_specs=[
            pl.BlockSpec((t, d), lambda i: (i, 0)),
            pl.BlockSpec((t, SWA_Q_W), lambda i: (i, 0)),
            pl.BlockSpec((t, FOX_W), lambda i: (i, 0)),
            const(wg.shape), const(wps.shape), const(wpf.shape), const(wo.shape),
            const(g1.shape), const(b1.shape), const(wr.shape), const(br.shape),
        ],
        out_specs=[
            pl.BlockSpec((t, d), lambda i: (i, 0)),
            pl.BlockSpec((t * d // LANES, LANES), lambda i: (i, 0)),
            pl.BlockSpec((8, t), lambda i: (0, i)),
            pl.BlockSpec((t, LANES), lambda i: (i, 0)),
        ],
        out_shape=[
            jax.ShapeDtypeStruct((n, d), _F32),
            jax.ShapeDtypeStruct((n * d // LANES, LANES), _F32),
            jax.ShapeDtypeStruct((8, n), jnp.int32),
            jax.ShapeDtypeStruct((n, LANES), _F32),
        ],
        compiler_params=_params("arbitrary"),
        name="mix",
    )(x2, att_a, att_f, wg, wps, wpf, wo, g1, b1, wr, br)


def _route_kernel(idx_ref, tri_ref, slot_ref, meta_ref, cnt_ref, run_ref):
    phase = pl.program_id(0)
    i = pl.program_id(1)
    t = idx_ref.shape[1]
    blk = EXPERT_ROWS
    eidx = lax.broadcasted_iota(jnp.int32, (N_EXPERTS, t), 0)
    idx = idx_ref[...]
    sel = (eidx == idx[0:1, :])
    for k in range(1, TOP_K):
        sel = sel | (eidx == idx[k:k + 1, :])
    self32 = jnp.where(sel, 1.0, 0.0)
    tile_cnt = jnp.sum(self32, axis=1, keepdims=True)

    @pl.when((phase == 0) & (i == 0))
    def _():
        cnt_ref[...] = jnp.zeros_like(cnt_ref)

    @pl.when(phase == 0)
    def _():
        cnt_ref[...] = cnt_ref[...] + tile_cnt

    @pl.when((phase == 1) & (i == 0))
    def _():
        cnt = cnt_ref[...]
        padded = jnp.floor((cnt + (blk - 1)) * (1.0 / blk)) * blk
        run = jnp.zeros((1, LANES), _F32)
        rows = []
        for e in range(N_EXPERTS):
            rows.append(run)
            run = run + padded[e:e + 1, :]
        start = jnp.concatenate(rows, axis=0)
        run_ref[...] = start
        nbp = meta_ref.shape[1]
        s1 = start[:, 0:1]
        e1 = s1 + padded[:, 0:1]
        c1 = cnt[:, 0:1]
        row0 = (lax.broadcasted_iota(jnp.int32, (N_EXPERTS, nbp), 1) * blk).astype(_F32)
        owner = jnp.sum(jnp.where(e1 <= row0, 1.0, 0.0), axis=0, keepdims=True)
        owner = jnp.minimum(owner, N_EXPERTS - 1.0)
        inside = (s1 <= row0) & (row0 < e1)
        nval = jnp.sum(jnp.where(inside, jnp.clip(c1 - (row0 - s1), 0.0, blk), 0.0), axis=0, keepdims=True)
        erow = lax.broadcasted_iota(jnp.int32, (N_EXPERTS, nbp), 0).astype(_F32)
        later = (erow > owner) & (c1 > 0.0)
        nxt = jnp.min(jnp.where(later, erow, float(N_EXPERTS)), axis=0, keepdims=True)
        meta_ref[...] = jnp.concatenate(
            [owner.astype(jnp.int32), nval.astype(jnp.int32), nxt.astype(jnp.int32),
             jnp.zeros((5, nbp), jnp.int32)], axis=0)

    @pl.when(phase == 1)
    def _():
        rank = _dot(self32.astype(_BF16), tri_ref[...])
        slot = (run_ref[:, 0:1] + rank).astype(jnp.int32)
        for k in range(TOP_K):
            sk = jnp.sum(jnp.where(eidx == idx[k:k + 1, :], slot, 0), axis=0, keepdims=True)
            slot_ref[0, :, k * t:(k + 1) * t] = sk
        run_ref[...] = run_ref[...] + tile_cnt


def _route(idx_t, tri_strict, n_blocks_pad):
    n = idx_t.shape[1]
    t = SEQ_TILE
    nt = n // t
    return pl.pallas_call(
        _route_kernel,
        grid=(2, nt),
        in_specs=[
            pl.BlockSpec((8, t), lambda ph, i: (0, i)),
            pl.BlockSpec((t, t), lambda ph, i: (0, 0)),
        ],
        out_specs=[
            pl.BlockSpec((1, 1, TOP_K * t), lambda ph, i: (ph * i, 0, 0)),
            pl.BlockSpec((8, n_blocks_pad), lambda ph, i: (0, 0)),
        ],
        out_shape=[
            jax.ShapeDtypeStruct((nt, 1, TOP_K * t), jnp.int32),
            jax.ShapeDtypeStruct((8, n_blocks_pad), jnp.int32),
        ],
        scratch_shapes=[
            pltpu.VMEM((N_EXPERTS, LANES), _F32),
            pltpu.VMEM((N_EXPERTS, LANES), _F32),
        ],
        compiler_params=_params("arbitrary", "arbitrary"),
        name="route",
    )(idx_t, tri_strict)


SLAB = 8
DRAIN_UNROLL = 8


def _row_copy(src_ref, src_row, dst_ref, dst_row, sem):
    src = src_ref.at[pl.ds(pl.multiple_of(src_row * SLAB, SLAB), SLAB), :]
    dst = dst_ref.at[pl.ds(pl.multiple_of(dst_row * SLAB, SLAB), SLAB), :]
    return pltpu.make_async_copy(src, dst, sem)


def _dispatch_kernel(meta_ref, slot_ref, h_ref, xs_ref, zero_buf, sem, zsem):
    t = h_ref.shape[0] // SLAB
    blk = zero_buf.shape[0]

    @pl.when(pl.program_id(0) == 0)
    def _():
        zero_buf[...] = jnp.zeros_like(zero_buf)
        n_blocks = xs_ref.shape[0] // blk

        def fill(bi):
            return pltpu.make_async_copy(zero_buf, xs_ref.at[pl.ds(pl.multiple_of(bi * blk, blk), blk), :], zsem)

        def start(bi, carry):
            @pl.when(meta_ref[1, bi] < EXPERT_ROWS)
            def _():
                fill(bi).start()
            return carry

        def wait(bi, carry):
            @pl.when(meta_ref[1, bi] < EXPERT_ROWS)
            def _():
                fill(bi).wait()
            return carry

        lax.fori_loop(0, n_blocks, start, 0)
        lax.fori_loop(0, n_blocks, wait, 0)

    def issue(r, carry):
        for k in range(TOP_K):
            _row_copy(h_ref, r, xs_ref, slot_ref[0, 0, k * t + r], sem).start(priority=k % 2)
        return carry

    lax.fori_loop(0, t, issue, 0)

    def drain(r8, carry):
        for u in range(DRAIN_UNROLL):
            for k in range(TOP_K):
                _row_copy(h_ref, r8 * DRAIN_UNROLL + u, xs_ref, 0, sem).wait()
        return carry

    lax.fori_loop(0, t // DRAIN_UNROLL, drain, 0)


def _dispatch(meta, slots, hs, n_rows):
    n = hs.shape[0] // SLAB
    t = SEQ_TILE
    grid_spec = pltpu.PrefetchScalarGridSpec(
        num_scalar_prefetch=1,
        grid=(n // t,),
        in_specs=[
            pl.BlockSpec((1, 1, TOP_K * t), lambda i, m: (i, 0, 0), memory_space=pltpu.SMEM),
            pl.BlockSpec((t * SLAB, LANES), lambda i, m: (i, 0)),
        ],
        out_specs=pl.BlockSpec(memory_space=pl.ANY),
        scratch_shapes=[pltpu.VMEM((EXPERT_ROWS * SLAB, LANES), _F32), pltpu.SemaphoreType.DMA(()),
                        pltpu.SemaphoreType.DMA(())],
    )
    return pl.pallas_call(
        _dispatch_kernel,
        grid_spec=grid_spec,
        out_shape=jax.ShapeDtypeStruct((n_rows * SLAB, LANES), _F32),
        compiler_params=_params("arbitrary"),
        name="dispatch",
    )(meta, slots, hs)


def _expert_kernel(meta_ref, x_ref, wgu_hbm, bgu_ref, wd_hbm, bd_ref, y_ref,
                   wgu_land, wd_land, wgu_bf, wd_bf, sems, turn_ref):
    b = pl.program_id(0)
    e = meta_ref[0, b]
    nval = meta_ref[1, b]
    prev = meta_ref[0, jnp.maximum(b - 1, 0)]
    f = wd_bf.shape[0]

    def fetch(expert, slot):
        return (pltpu.make_async_copy(wgu_hbm.at[expert], wgu_land.at[slot], sems.at[0, slot]),
                pltpu.make_async_copy(wd_hbm.at[expert], wd_land.at[slot], sems.at[1, slot]))

    @pl.when(b == 0)
    def _():
        turn_ref[0] = 0
        for cp in fetch(e, 0):
            cp.start()

    @pl.when((nval > 0) & ((b == 0) | (prev != e)))
    def _():
        slot = turn_ref[0]
        for cp in fetch(e, slot):
            cp.wait()
        nxt = meta_ref[2, b]

        @pl.when(nxt < N_EXPERTS)
        def _():
            for cp in fetch(nxt, 1 - slot):
                cp.start(priority=1)

        wgu_bf[...] = wgu_land[slot].astype(_BF16)
        wd_bf[...] = wd_land[slot].astype(_BF16)
        turn_ref[0] = 1 - slot

    @pl.when(nval > 0)
    def _():
        x = _load_slabs(x_ref, EXPERT_ROWS, SLAB).astype(_BF16)
        hgu = _dot(x, wgu_bf[...]) + bgu_ref[0]
        gate = jnp.minimum(hgu[:, :f], SWIGLU_LIMIT)
        up = jnp.clip(hgu[:, f:], -SWIGLU_LIMIT, SWIGLU_LIMIT)
        act = gate * jax.nn.sigmoid(gate * SWIGLU_ALPHA) * (up + 1.0)
        _store_slabs(y_ref, _dot(act.astype(_BF16), wd_bf[...]) + bd_ref[0])

    @pl.when(nval <= 0)
    def _():
        y_ref[...] = jnp.zeros_like(y_ref)


def _experts(meta, xs, w_gu, b_gu, w_d, b_d):
    ne, d, f2 = w_gu.shape
    assert d == SLAB * LANES
    r = xs.shape[0] // SLAB
    f = w_d.shape[1]
    blk = EXPERT_ROWS
    grid_spec = pltpu.PrefetchScalarGridSpec(
        num_scalar_prefetch=1,
        grid=(r // blk,),
        in_specs=[
            pl.BlockSpec((blk * SLAB, LANES), lambda i, m: (i, 0)),
            pl.BlockSpec(memory_space=pl.ANY),
            pl.BlockSpec((1, 1, f2), lambda i, m: (m[0, i], 0, 0)),
            pl.BlockSpec(memory_space=pl.ANY),
            pl.BlockSpec((1, 1, d), lambda i, m: (m[0, i], 0, 0)),
        ],
        out_specs=pl.BlockSpec((blk * SLAB, LANES), lambda i, m: (i, 0)),
        scratch_shapes=[
            pltpu.VMEM((2, d, f2), _F32), pltpu.VMEM((2, f, d), _F32),
            pltpu.VMEM((d, f2), _BF16), pltpu.VMEM((f, d), _BF16),
            pltpu.SemaphoreType.DMA((2, 2)), pltpu.SMEM((1,), jnp.int32),
        ],
    )
    return pl.pallas_call(
        _expert_kernel,
        grid_spec=grid_spec,
        out_shape=jax.ShapeDtypeStruct((r * SLAB, LANES), _F32),
        compiler_params=_params("arbitrary"),
        name="experts",
    )(meta, xs, w_gu, b_gu.reshape(ne, 1, f2), w_d, b_d.reshape(ne, 1, d))


def _combine_kernel(alpha, slot_ref, h_ref, gate_ref, g2_ref, b2_ref, ys_ref, o_ref, buf, sem):
    t = h_ref.shape[0]

    def issue(r, carry):
        for k in range(TOP_K):
            _row_copy(ys_ref, slot_ref[0, 0, k * t + r], buf.at[k], r, sem).start(priority=k % 2)
        return carry

    lax.fori_loop(0, t, issue, 0)

    def drain(r8, carry):
        for u in range(DRAIN_UNROLL):
            for k in range(TOP_K):
                _row_copy(ys_ref, 0, buf.at[k], r8 * DRAIN_UNROLL + u, sem).wait()
        return carry

    lax.fori_loop(0, t // DRAIN_UNROLL, drain, 0)

    gate = gate_ref[...]
    moe = gate[:, 0:1] * _load_slabs(buf, t, SLAB, (0,))
    for k in range(1, TOP_K):
        moe = moe + gate[:, k:k + 1] * _load_slabs(buf, t, SLAB, (k,))
    o_ref[...] = _layer_norm(alpha * h_ref[...] + moe, g2_ref[...], b2_ref[...])


def _combine(alpha, slots, h, gate, g2, b2, ys):
    n, d = h.shape
    t = SEQ_TILE
    return pl.pallas_call(
        functools.partial(_combine_kernel, alpha),
        grid=(n // t,),
        in_specs=[
            pl.BlockSpec((1, 1, TOP_K * t), lambda i: (i, 0, 0), memory_space=pltpu.SMEM),
            pl.BlockSpec((t, d), lambda i: (i, 0)),
            pl.BlockSpec((t, LANES), lambda i: (i, 0)),
            pl.BlockSpec((1, d), lambda i: (0, 0)),
            pl.BlockSpec((1, d), lambda i: (0, 0)),
            pl.BlockSpec(memory_space=pl.ANY),
        ],
        out_specs=pl.BlockSpec((t, d), lambda i: (i, 0)),
        out_shape=jax.ShapeDtypeStruct((n, d), _F32),
        scratch_shapes=[pltpu.VMEM((TOP_K, t * SLAB, LANES), _F32), pltpu.SemaphoreType.DMA(())],
        compiler_params=_params("arbitrary"),
        name="combine",
    )(slots, h, gate, g2, b2, ys)


def _pad_cols(a, width):
    return jnp.pad(a, ((0, 0), (0, width - a.shape[1])))


def kernel(x, w_in, b_forget, sink, w_proj_swa, w_proj_fox, w_out, ln1_g, ln1_b, w_router, b_router,
           w_gate_up, b_gate_up, w_down, b_down, ln2_g, ln2_b):
    b, s, d = x.shape
    depth = w_in.shape[0]
    n = b * s
    alpha = float((2.0 * depth) ** 0.25)
    assert s % SEQ_TILE == 0 and SEQ_TILE % WINDOW == 0 and d % LANES == 0
    assert w_router.shape[2] == N_EXPERTS and w_in.shape[2] == ATT_W + FOX_HEADS + 2 * d

    blk = EXPERT_ROWS
    n_rows = ((n * TOP_K + blk - 1) // blk) * blk + N_EXPERTS * blk
    n_blocks_pad = ((n_rows // blk + LANES - 1) // LANES) * LANES
    t = SEQ_TILE
    ri = lax.broadcasted_iota(jnp.int32, (t, t), 0)
    ci = lax.broadcasted_iota(jnp.int32, (t, t), 1)
    tri_incl = (ci <= ri).astype(_BF16)
    tri_strict = (ri < ci).astype(_BF16)

    h = x
    for layer in range(depth):
        w_l = w_in[layer]
        w_att = _pad_cols(w_l[:, :ATT_W + FOX_HEADS], ATT_W + LANES).astype(_BF16)
        w_gates = w_l[:, ATT_W + FOX_HEADS:].astype(_BF16)
        bf_pad = _pad_cols(b_forget[layer].reshape(1, FOX_HEADS).astype(_F32), LANES)

        qkv, ccol, kaug = _in_proj(h, w_att, bf_pad, tri_incl)
        att_a = _swa(qkv, sink[layer].reshape(-1).astype(_F32))
        att_f = _fox(qkv, ccol, kaug)

        h1, h1_slabs, idx_t, gate = _mix(
            alpha, h.reshape(n, d), att_a.reshape(n, SWA_Q_W), att_f.reshape(n, FOX_W),
            w_gates, w_proj_swa[layer].astype(_BF16), w_proj_fox[layer].astype(_BF16),
            w_out[layer].astype(_BF16), ln1_g[layer].reshape(1, d), ln1_b[layer].reshape(1, d),
            _pad_cols(w_router[layer], LANES).astype(_BF16),
            _pad_cols(b_router[layer].reshape(1, N_EXPERTS).astype(_F32), LANES))

        slots, meta = _route(idx_t, tri_strict, n_blocks_pad)
        xs = _dispatch(meta, slots, h1_slabs, n_rows)
        ys = _experts(meta, xs, w_gate_up[layer], b_gate_up[layer], w_down[layer], b_down[layer])
        out = _combine(alpha, slots, h1, gate, ln2_g[layer].reshape(1, d), ln2_b[layer].reshape(1, d), ys)
        h = out.reshape(b, s, d)
    return h
```

```python
import functools

import numpy as np
import jax
import jax.numpy as jnp
from jax import lax
from jax.experimental import pallas as pl
from jax.experimental.pallas import tpu as pltpu

HEAD_DIM = 64
SWA_Q_HEADS = 8
SWA_KV_HEADS = 2
SWA_GROUP = SWA_Q_HEADS // SWA_KV_HEADS
WINDOW = 128
FOX_HEADS = 8
N_EXPERTS = 32
TOP_K = 4
SWIGLU_LIMIT = 7.0
SWIGLU_ALPHA = 1.702
LN_EPS = 1e-5

LANES = 128
SEQ_TILE = 512
EXPERT_ROWS = 512
CAST_ROWS = 64
MIX_SPLIT = 2
FOX_ROW_CHUNK = 64
VMEM_LIMIT = 56 * 1024 * 1024

SWA_Q_W = SWA_Q_HEADS * HEAD_DIM
SWA_KV_W = SWA_KV_HEADS * HEAD_DIM
FOX_W = FOX_HEADS * HEAD_DIM
ATT_W = SWA_Q_W + 2 * SWA_KV_W + 3 * FOX_W
NEG_INF = float("-inf")
LOG2E = 1.4426950408889634
FOX_Q_SCALE = HEAD_DIM ** -0.5 * LOG2E

_F32 = jnp.float32
_BF16 = jnp.bfloat16


def _dot(a, b):
    return jnp.dot(a, b, preferred_element_type=_F32)


def _dot_nt(a, b):
    return lax.dot_general(a, b, (((1,), (1,)), ((), ())), preferred_element_type=_F32)


def _params(*sem):
    return pltpu.CompilerParams(dimension_semantics=sem, vmem_limit_bytes=VMEM_LIMIT)


def _layer_norm(z, g, b):
    mu = jnp.mean(z, axis=-1, keepdims=True)
    zc = z - mu
    var = jnp.mean(zc * zc, axis=-1, keepdims=True)
    return zc * lax.rsqrt(var + LN_EPS) * g + b


def _store_slabs(ref, value, lead=()):
    rows, width = value.shape
    c = width // LANES
    for j in range(c):
        ref[lead + (pl.ds(j, rows, stride=c), slice(None))] = value[:, j * LANES:(j + 1) * LANES]


def _load_slabs(ref, rows, c, lead=()):
    return jnp.concatenate([ref[lead + (pl.ds(j, rows, stride=c), slice(None))] for j in range(c)], axis=1)


def _split3(v):
    p1 = v.astype(_BF16).astype(_F32)
    r1 = v - p1
    p2 = r1.astype(_BF16).astype(_F32)
    p3 = (r1 - p2).astype(_BF16).astype(_F32)
    return p1, p2, p3


def _in_proj_kernel(x_ref, w_ref, bf_ref, tri_ref, qkv_ref, ccol_ref, kaug_ref, carry_ref):
    @pl.when(pl.program_id(1) == 0)
    def _():
        carry_ref[...] = jnp.zeros_like(carry_ref)

    xb = x_ref[0].astype(_BF16)
    acc = _dot(xb, w_ref[...])
    qf_col = SWA_Q_W + 2 * SWA_KV_W
    qkv_ref[0, :, :qf_col] = acc[:, :qf_col].astype(_BF16)
    qkv_ref[0, :, qf_col:qf_col + FOX_W] = (acc[:, qf_col:qf_col + FOX_W] * FOX_Q_SCALE).astype(_BF16)
    qkv_ref[0, :, qf_col + FOX_W:] = acc[:, qf_col + FOX_W:ATT_W].astype(_BF16)
    kf_col = SWA_Q_W + 2 * SWA_KV_W + FOX_W
    kt = acc[:, kf_col:kf_col + FOX_W].T
    z = acc[:, ATT_W:] + bf_ref[...]
    log_f = jnp.minimum(z, 0.0) - jnp.log1p(jnp.exp(-jnp.abs(z)))
    tri = tri_ref[...]
    p1 = log_f.astype(_BF16)
    r1 = log_f - p1.astype(_F32)
    p2 = r1.astype(_BF16)
    p3 = (r1 - p2.astype(_F32)).astype(_BF16)
    c = _dot(tri, p1) + _dot(tri, p2) + _dot(tri, p3) + carry_ref[0:1, :]
    t = c.shape[0]
    carry_ref[...] = jnp.broadcast_to(c[t - 1:t, :], carry_ref.shape)
    cl = c * LOG2E
    ccol_ref[0] = cl
    ct = cl.T
    row = lax.broadcasted_iota(jnp.int32, (8, t), 0)
    ones_rows = jnp.where(row == 0, 1.0, 0.0)
    pad_rows = jnp.zeros((HEAD_DIM - 16, t), _F32)
    for h in range(FOX_HEADS):
        c1, c2, c3 = _split3(ct[h:h + 1, :])
        ck_rows = jnp.where(row == 0, -c1, jnp.where(row == 1, -c2, jnp.where(row == 2, -c3, 0.0)))
        bias_half = jnp.concatenate([ones_rows, ck_rows, pad_rows], axis=0)
        kh = kt[h * HEAD_DIM:(h + 1) * HEAD_DIM, :]
        halves = [kh, bias_half] if h % 2 == 0 else [bias_half, kh]
        kaug_ref[0, h, 0] = jnp.concatenate(halves, axis=0).astype(_BF16)


def _in_proj(x, w_att, bf_pad, tri):
    b, s, d = x.shape
    t = SEQ_TILE
    wn = w_att.shape[1]
    return pl.pallas_call(
        _in_proj_kernel,
        grid=(b, s // t),
        in_specs=[
            pl.BlockSpec((1, t, d), lambda i, j: (i, j, 0)),
            pl.BlockSpec((d, wn), lambda i, j: (0, 0)),
            pl.BlockSpec((1, LANES), lambda i, j: (0, 0)),
            pl.BlockSpec((t, t), lambda i, j: (0, 0)),
        ],
        out_specs=[
            pl.BlockSpec((1, t, ATT_W), lambda i, j: (i, j, 0)),
            pl.BlockSpec((1, t, LANES), lambda i, j: (i, j, 0)),
            pl.BlockSpec((1, FOX_HEADS, 1, LANES, t), lambda i, j: (i, 0, j, 0, 0)),
        ],
        out_shape=[
            jax.ShapeDtypeStruct((b, s, ATT_W), _BF16),
            jax.ShapeDtypeStruct((b, s, LANES), _F32),
            jax.ShapeDtypeStruct((b, FOX_HEADS, s // t, LANES, t), _BF16),
        ],
        scratch_shapes=[pltpu.VMEM((8, LANES), _F32)],
        compiler_params=_params("arbitrary", "arbitrary"),
        name="in_proj",
    )(x, w_att, bf_pad, tri)


def _alibi_slopes(n):
    return [float(v) for v in np.asarray(2.0 ** (-8.0 * np.arange(1, n + 1) / n), dtype=np.float32)]


def _swa_kernel(sink_ref, q_ref, kp_ref, kc_ref, vp_ref, vc_ref, o_ref):
    blk = pl.program_id(1)
    w = WINDOW
    q = q_ref[0] * jnp.asarray(HEAD_DIM ** -0.5, _BF16)
    k = jnp.concatenate([kp_ref[0], kc_ref[0]], axis=0)
    v = jnp.concatenate([vp_ref[0], vc_ref[0]], axis=0)
    row = lax.broadcasted_iota(jnp.int32, (w, 2 * w), 0)
    col = lax.broadcasted_iota(jnp.int32, (w, 2 * w), 1)
    dist = row + w - col
    valid = (dist >= 0) & (dist < w) & ((col >= w) | (blk > 0))
    distf = dist.astype(_F32)
    slopes = _alibi_slopes(SWA_Q_HEADS)
    outs = []
    for h in range(SWA_Q_HEADS):
        g = h // SWA_GROUP
        qh = q[:, h * HEAD_DIM:(h + 1) * HEAD_DIM]
        kg = k[:, g * HEAD_DIM:(g + 1) * HEAD_DIM]
        vg = v[:, g * HEAD_DIM:(g + 1) * HEAD_DIM]
        sc = _dot_nt(qh, kg) - slopes[h] * distf
        sc = jnp.where(valid, sc, NEG_INF)
        sk = sink_ref[h]
        m = jnp.maximum(jnp.max(sc, axis=1, keepdims=True), sk)
        p = jnp.exp(sc - m)
        denom = jnp.sum(p, axis=1, keepdims=True) + jnp.exp(sk - m)
        o = _dot(p.astype(_BF16), vg)
        outs.append(o / denom)
    o_ref[0] = jnp.concatenate(outs, axis=1).astype(_BF16)


def _swa(qkv, sink_flat):
    b, s, _ = qkv.shape
    w = WINDOW
    kcol = SWA_Q_W // LANES
    vcol = kcol + SWA_KV_W // LANES
    grid_spec = pltpu.PrefetchScalarGridSpec(
        num_scalar_prefetch=1,
        grid=(b, s // w),
        in_specs=[
            pl.BlockSpec((1, w, SWA_Q_W), lambda i, j, sk: (i, j, 0)),
            pl.BlockSpec((1, w, SWA_KV_W), lambda i, j, sk: (i, jnp.maximum(j - 1, 0), kcol)),
            pl.BlockSpec((1, w, SWA_KV_W), lambda i, j, sk: (i, j, kcol)),
            pl.BlockSpec((1, w, SWA_KV_W), lambda i, j, sk: (i, jnp.maximum(j - 1, 0), vcol)),
            pl.BlockSpec((1, w, SWA_KV_W), lambda i, j, sk: (i, j, vcol)),
        ],
        out_specs=pl.BlockSpec((1, w, SWA_Q_W), lambda i, j, sk: (i, j, 0)),
    )
    return pl.pallas_call(
        _swa_kernel,
        grid_spec=grid_spec,
        out_shape=jax.ShapeDtypeStruct((b, s, SWA_Q_W), _BF16),
        compiler_params=_params("arbitrary", "arbitrary"),
        name="swa",
    )(sink_flat, qkv, qkv, qkv, qkv, qkv)


def _fox_kernel(q_ref, kaug_ref, v_ref, ccol_ref, o_ref, m_ref, acc_ref, s_ref, p_ref):
    pair = pl.program_id(1)
    t = SEQ_TILE
    nq = q_ref.shape[1] // t
    lane = lax.broadcasted_iota(jnp.int32, (t, LANES), 1)
    low = lane < HEAD_DIM
    rc = FOX_ROW_CHUNK
    low_c = lax.broadcasted_iota(jnp.int32, (rc, LANES), 1) < HEAD_DIM
    ones0 = jnp.where(low, 1.0, 0.0).astype(_BF16)
    ones1 = jnp.where(low, 0.0, 1.0).astype(_BF16)

    def q_aug(qi):
        q = q_ref[0, qi * t:(qi + 1) * t, :].astype(_F32)
        ccol = ccol_ref[0, qi * t:(qi + 1) * t, :]
        qa = []
        for hh in range(2):
            cq = jnp.sum(jnp.where(lane == 2 * pair + hh, ccol, 0.0), axis=1, keepdims=True)
            base = HEAD_DIM if hh == 0 else 0
            bias = jnp.where(lane == base, cq, 0.0)
            bias = jnp.where((lane >= base + 8) & (lane < base + 11), 1.0, bias)
            qa.append(jnp.where(low if hh == 0 else ~low, q, bias).astype(_BF16))
        return qa

    def logits(slot, qa, j):
        for hh in range(2):
            s_ref[slot, hh] = _dot(qa[hh], kaug_ref[0, hh, j])

    def absorb(slot, j, masked):
        v = v_ref[0, j * t:(j + 1) * t, :]
        zero = jnp.zeros_like(v)
        vcat = jnp.concatenate([jnp.concatenate([jnp.where(low, v, zero), ones0], axis=1),
                                jnp.concatenate([jnp.where(low, zero, v), ones1], axis=1)], axis=0)
        for r in range(t // rc):
            rows = pl.ds(r * rc, rc)
            if masked:
                keep = (lax.broadcasted_iota(jnp.int32, (rc, t), 1)
                        <= lax.broadcasted_iota(jnp.int32, (rc, t), 0) + r * rc)
            alphas = []
            for hh in range(2):
                s = s_ref[slot, hh, rows, :]
                if masked:
                    s = jnp.where(keep, s, NEG_INF)
                m_old = m_ref[hh, rows, :]
                m_new = jnp.maximum(m_old, jnp.max(s, axis=1, keepdims=True))
                alpha = jnp.exp2(m_old - m_new)
                p = jnp.exp2(s - jnp.tile(m_new, (1, t // LANES)))
                m_ref[hh, rows, :] = m_new
                p_ref[rows, hh * t:(hh + 1) * t] = p.astype(_BF16)
                alphas.append(alpha)
            scale = jnp.where(low_c, alphas[0], alphas[1])
            acc_ref[rows, :] = jnp.concatenate([scale, scale], axis=1) * acc_ref[rows, :]
        acc_ref[...] = acc_ref[...] + _dot(p_ref[...], vcat)

    visits = [(qi, j) for qi in range(nq) for j in range(qi + 1)]
    qa = q_aug(0)
    logits(0, qa, 0)
    for u, (qi, j) in enumerate(visits):
        slot = u % 2
        if j == 0:
            m_ref[...] = jnp.full_like(m_ref, NEG_INF)
            acc_ref[...] = jnp.zeros_like(acc_ref)
        if u + 1 < len(visits):
            nqi, nj = visits[u + 1]
            if nj == 0:
                qa = q_aug(nqi)
            logits(1 - slot, qa, nj)
        absorb(slot, j, j == qi)
        if j == qi:
            acc = acc_ref[...]
            o_ref[0, qi * t:(qi + 1) * t, :] = (acc[:, :LANES] / acc[:, LANES:]).astype(_BF16)


def _fox(qkv, ccol, kaug):
    b, s, _ = qkv.shape
    t = SEQ_TILE
    pairs = FOX_HEADS // 2
    qcol = (SWA_Q_W + 2 * SWA_KV_W) // LANES
    vcol = qcol + 2 * FOX_W // LANES
    return pl.pallas_call(
        _fox_kernel,
        grid=(b, pairs),
        in_specs=[
            pl.BlockSpec((1, s, LANES), lambda i, p: (i, 0, qcol + p)),
            pl.BlockSpec((1, 2, s // t, LANES, t), lambda i, p: (i, p, 0, 0, 0)),
            pl.BlockSpec((1, s, LANES), lambda i, p: (i, 0, vcol + p)),
            pl.BlockSpec((1, s, LANES), lambda i, p: (i, 0, 0)),
        ],
        out_specs=pl.BlockSpec((1, s, LANES), lambda i, p: (i, 0, p)),
        out_shape=jax.ShapeDtypeStruct((b, s, FOX_W), _BF16),
        scratch_shapes=[
            pltpu.VMEM((2, t, LANES), _F32),
            pltpu.VMEM((t, 2 * LANES), _F32),
            pltpu.VMEM((2, 2, t, t), _F32),
            pltpu.VMEM((t, 2 * t), _BF16),
        ],
        compiler_params=_params("arbitrary", "arbitrary"),
        name="fox",
    )(qkv, kaug, qkv, ccol)


def _mix_kernel(alpha, x_ref, aa_ref, af_ref, wg_ref, wps_ref, wpf_ref, wo_ref, g1_ref, b1_ref,
                wr_ref, br_ref, h_ref, hs_ref, idx_ref, gate_ref):
    d = x_ref.shape[1]
    t = x_ref.shape[0] // MIX_SPLIT
    for part in range(MIX_SPLIT):
        rows = slice(part * t, (part + 1) * t)
        x = x_ref[rows, :]
        gates = _dot(x.astype(_BF16), wg_ref[...])
        ya = _dot(aa_ref[rows, :], wps_ref[...])
        yf = _dot(af_ref[rows, :], wpf_ref[...])
        mix = jax.nn.sigmoid(gates[:, :d]) * ya + jax.nn.sigmoid(gates[:, d:]) * yf
        z = alpha * x + _dot(mix.astype(_BF16), wo_ref[...])
        h = _layer_norm(z, g1_ref[...], b1_ref[...])
        h_ref[rows, :] = h
        c = d // LANES
        for j in range(c):
            hs_ref[pl.ds(part * t * c + j, t, stride=c), :] = h[:, j * LANES:(j + 1) * LANES]

        logits = _dot(h.astype(_BF16), wr_ref[...]) + br_ref[...]
        lt = logits.T[:N_EXPERTS, :]
        eidx = lax.broadcasted_iota(jnp.int32, (N_EXPERTS, t), 0)
        work = lt
        vals, idxs = [], []
        for _ in range(TOP_K):
            mk = jnp.max(work, axis=0, keepdims=True)
            ik = jnp.min(jnp.where(work == mk, eidx, N_EXPERTS), axis=0, keepdims=True)
            vals.append(mk)
            idxs.append(ik)
            work = jnp.where(eidx == ik, NEG_INF, work)
        ex = [jnp.exp(vk - vals[0]) for vk in vals]
        denom = ex[0] + ex[1] + ex[2] + ex[3]
        idx_ref[:, rows] = jnp.concatenate(idxs + [jnp.zeros((8 - TOP_K, t), jnp.int32)], axis=0)
        gt = jnp.concatenate([e / denom for e in ex] + [jnp.zeros((LANES - TOP_K, t), _F32)], axis=0)
        gate_ref[rows, :] = gt.T


def _mix(alpha, x2, att_a, att_f, wg, wps, wpf, wo, g1, b1, wr, br):
    n, d = x2.shape
    t = SEQ_TILE
    const = lambda shape: pl.BlockSpec(shape, lambda i: (0, 0))
    return pl.pallas_call(
        functools.partial(_mix_kernel, alpha),
        grid=(n // t,),
        in_specs=[
            pl.BlockSpec((t, d), lambda i: (i, 0)),
            pl.BlockSpec((t, SWA_Q_W), lambda i: (i, 0)),
            pl.BlockSpec((t, FOX_W), lambda i: (i, 0)),
            const(wg.shape), const(wps.shape), const(wpf.shape), const(wo.shape),
            const(g1.shape), const(b1.shape), const(wr.shape), const(br.shape),
        ],
        out_specs=[
            pl.BlockSpec((t, d), lambda i: (i, 0)),
            pl.BlockSpec((t * d // LANES, LANES), lambda i: (i, 0)),
            pl.BlockSpec((8, t), lambda i: (0, i)),
            pl.BlockSpec((t, LANES), lambda i: (i, 0)),
        ],
        out_shape=[
            jax.ShapeDtypeStruct((n, d), _F32),
            jax.ShapeDtypeStruct((n * d // LANES, LANES), _F32),
            jax.ShapeDtypeStruct((8, n), jnp.int32),
            jax.ShapeDtypeStruct((n, LANES), _F32),
        ],
        compiler_params=_params("arbitrary"),
        name="mix",
    )(x2, att_a, att_f, wg, wps, wpf, wo, g1, b1, wr, br)


def _route_kernel(idx_ref, tri_ref, slot_ref, meta_ref, cnt_ref, run_ref):
    phase = pl.program_id(0)
    i = pl.program_id(1)
    t = idx_ref.shape[1]
    blk = EXPERT_ROWS
    eidx = lax.broadcasted_iota(jnp.int32, (N_EXPERTS, t), 0)
    idx = idx_ref[...]
    sel = (eidx == idx[0:1, :])
    for k in range(1, TOP_K):
        sel = sel | (eidx == idx[k:k + 1, :])
    self32 = jnp.where(sel, 1.0, 0.0)
    tile_cnt = jnp.sum(self32, axis=1, keepdims=True)

    @pl.when((phase == 0) & (i == 0))
    def _():
        cnt_ref[...] = jnp.zeros_like(cnt_ref)

    @pl.when(phase == 0)
    def _():
        cnt_ref[...] = cnt_ref[...] + tile_cnt

    @pl.when((phase == 1) & (i == 0))
    def _():
        cnt = cnt_ref[...]
        padded = jnp.floor((cnt + (blk - 1)) * (1.0 / blk)) * blk
        run = jnp.zeros((1, LANES), _F32)
        rows = []
        for e in range(N_EXPERTS):
            rows.append(run)
            run = run + padded[e:e + 1, :]
        start = jnp.concatenate(rows, axis=0)
        run_ref[...] = start
        nbp = meta_ref.shape[1]
        s1 = start[:, 0:1]
        e1 = s1 + padded[:, 0:1]
        c1 = cnt[:, 0:1]
        row0 = (lax.broadcasted_iota(jnp.int32, (N_EXPERTS, nbp), 1) * blk).astype(_F32)
        owner = jnp.sum(jnp.where(e1 <= row0, 1.0, 0.0), axis=0, keepdims=True)
        owner = jnp.minimum(owner, N_EXPERTS - 1.0)
        inside = (s1 <= row0) & (row0 < e1)
        nval = jnp.sum(jnp.where(inside, jnp.clip(c1 - (row0 - s1), 0.0, blk), 0.0), axis=0, keepdims=True)
        erow = lax.broadcasted_iota(jnp.int32, (N_EXPERTS, nbp), 0).astype(_F32)
        later = (erow > owner) & (c1 > 0.0)
        nxt = jnp.min(jnp.where(later, erow, float(N_EXPERTS)), axis=0, keepdims=True)
        meta_ref[...] = jnp.concatenate(
            [owner.astype(jnp.int32), nval.astype(jnp.int32), nxt.astype(jnp.int32),
             jnp.zeros((5, nbp), jnp.int32)], axis=0)

    @pl.when(phase == 1)
    def _():
        rank = _dot(self32.astype(_BF16), tri_ref[...])
        slot = (run_ref[:, 0:1] + rank).astype(jnp.int32)
        for k in range(TOP_K):
            sk = jnp.sum(jnp.where(eidx == idx[k:k + 1, :], slot, 0), axis=0, keepdims=True)
            slot_ref[0, :, k * t:(k + 1) * t] = sk
        run_ref[...] = run_ref[...] + tile_cnt


def _route(idx_t, tri_strict, n_blocks_pad):
    n = idx_t.shape[1]
    t = SEQ_TILE
    nt = n // t
    return pl.pallas_call(
        _route_kernel,
        grid=(2, nt),
        in_specs=[
            pl.BlockSpec((8, t), lambda ph, i: (0, i)),
            pl.BlockSpec((t, t), lambda ph, i: (0, 0)),
        ],
        out_specs=[
            pl.BlockSpec((1, 1, TOP_K * t), lambda ph, i: (ph * i, 0, 0)),
            pl.BlockSpec((8, n_blocks_pad), lambda ph, i: (0, 0)),
        ],
        out_shape=[
            jax.ShapeDtypeStruct((nt, 1, TOP_K * t), jnp.int32),
            jax.ShapeDtypeStruct((8, n_blocks_pad), jnp.int32),
        ],
        scratch_shapes=[
            pltpu.VMEM((N_EXPERTS, LANES), _F32),
            pltpu.VMEM((N_EXPERTS, LANES), _F32),
        ],
        compiler_params=_params("arbitrary", "arbitrary"),
        name="route",
    )(idx_t, tri_strict)


SLAB = 8
DRAIN_UNROLL = 8


def _row_copy(src_ref, src_row, dst_ref, dst_row, sem):
    src = src_ref.at[pl.ds(pl.multiple_of(src_row * SLAB, SLAB), SLAB), :]
    dst = dst_ref.at[pl.ds(pl.multiple_of(dst_row * SLAB, SLAB), SLAB), :]
    return pltpu.make_async_copy(src, dst, sem)


def _dispatch_kernel(meta_ref, slot_ref, h_ref, xs_ref, zero_buf, sem, zsem):
    t = h_ref.shape[0] // SLAB
    blk = zero_buf.shape[0]

    @pl.when(pl.program_id(0) == 0)
    def _():
        zero_buf[...] = jnp.zeros_like(zero_buf)
        n_blocks = xs_ref.shape[0] // blk

        def fill(bi):
            return pltpu.make_async_copy(zero_buf, xs_ref.at[pl.ds(pl.multiple_of(bi * blk, blk), blk), :], zsem)

        def start(bi, carry):
            @pl.when(meta_ref[1, bi] < EXPERT_ROWS)
            def _():
                fill(bi).start()
            return carry

        def wait(bi, carry):
            @pl.when(meta_ref[1, bi] < EXPERT_ROWS)
            def _():
                fill(bi).wait()
            return carry

        lax.fori_loop(0, n_blocks, start, 0)
        lax.fori_loop(0, n_blocks, wait, 0)

    def issue(r, carry):
        for k in range(TOP_K):
            _row_copy(h_ref, r, xs_ref, slot_ref[0, 0, k * t + r], sem).start(priority=k % 2)
        return carry

    lax.fori_loop(0, t, issue, 0)

    def drain(r8, carry):
        for u in range(DRAIN_UNROLL):
            for k in range(TOP_K):
                _row_copy(h_ref, r8 * DRAIN_UNROLL + u, xs_ref, 0, sem).wait()
        return carry

    lax.fori_loop(0, t // DRAIN_UNROLL, drain, 0)


def _dispatch(meta, slots, hs, n_rows):
    n = hs.shape[0] // SLAB
    t = SEQ_TILE
    grid_spec = pltpu.PrefetchScalarGridSpec(
        num_scalar_prefetch=1,
        grid=(n // t,),
        in_specs=[
            pl.BlockSpec((1, 1, TOP_K * t), lambda i, m: (i, 0, 0), memory_space=pltpu.SMEM),
            pl.BlockSpec((t * SLAB, LANES), lambda i, m: (i, 0)),
        ],
        out_specs=pl.BlockSpec(memory_space=pl.ANY),
        scratch_shapes=[pltpu.VMEM((EXPERT_ROWS * SLAB, LANES), _F32), pltpu.SemaphoreType.DMA(()),
                        pltpu.SemaphoreType.DMA(())],
    )
    return pl.pallas_call(
        _dispatch_kernel,
        grid_spec=grid_spec,
        out_shape=jax.ShapeDtypeStruct((n_rows * SLAB, LANES), _F32),
        compiler_params=_params("arbitrary"),
        name="dispatch",
    )(meta, slots, hs)


def _expert_kernel(meta_ref, x_ref, wgu_hbm, bgu_ref, wd_hbm, bd_ref, y_ref,
                   wgu_land, wd_land, wgu_bf, wd_bf, sems):
    b = pl.program_id(0)
    e = meta_ref[0, b]
    nval = meta_ref[1, b]
    prev = meta_ref[0, jnp.maximum(b - 1, 0)]
    f = wd_bf.shape[0]

    def fetch(expert):
        return (pltpu.make_async_copy(wgu_hbm.at[expert], wgu_land, sems.at[0]),
                pltpu.make_async_copy(wd_hbm.at[expert], wd_land, sems.at[1]))

    @pl.when(b == 0)
    def _():
        for cp in fetch(e):
            cp.start()

    @pl.when((nval > 0) & ((b == 0) | (prev != e)))
    def _():
        for cp in fetch(e):
            cp.wait()

        def convert(i, carry):
            rows = pl.ds(pl.multiple_of(i * CAST_ROWS, CAST_ROWS), CAST_ROWS)
            wgu_bf[rows, :] = wgu_land[rows, :].astype(_BF16)
            wd_bf[rows, :] = wd_land[rows, :].astype(_BF16)
            return carry

        lax.fori_loop(0, wgu_land.shape[0] // CAST_ROWS, convert, 0)
        nxt = meta_ref[2, b]

        @pl.when(nxt < N_EXPERTS)
        def _():
            for cp in fetch(nxt):
                cp.start(priority=1)

    def swiglu(rows):
        x = _load_slabs(x_ref, rows, SLAB).astype(_BF16)
        hgu = _dot(x, wgu_bf[...]) + bgu_ref[0]
        gate = jnp.minimum(hgu[:, :f], SWIGLU_LIMIT)
        up = jnp.clip(hgu[:, f:], -SWIGLU_LIMIT, SWIGLU_LIMIT)
        act = gate * jax.nn.sigmoid(gate * SWIGLU_ALPHA) * (up + 1.0)
        _store_slabs(y_ref, _dot(act.astype(_BF16), wd_bf[...]) + bd_ref[0])

    half = EXPERT_ROWS // 2

    @pl.when(nval > half)
    def _():
        swiglu(EXPERT_ROWS)

    @pl.when((nval > 0) & (nval <= half))
    def _():
        swiglu(half)
        y_ref[half * SLAB:, :] = jnp.zeros((half * SLAB, LANES), _F32)

    @pl.when(nval <= 0)
    def _():
        y_ref[...] = jnp.zeros_like(y_ref)


def _experts(meta, xs, w_gu, b_gu, w_d, b_d):
    ne, d, f2 = w_gu.shape
    assert d == SLAB * LANES
    r = xs.shape[0] // SLAB
    f = w_d.shape[1]
    blk = EXPERT_ROWS
    grid_spec = pltpu.PrefetchScalarGridSpec(
        num_scalar_prefetch=1,
        grid=(r // blk,),
        in_specs=[
            pl.BlockSpec((blk * SLAB, LANES), lambda i, m: (i, 0)),
            pl.BlockSpec(memory_space=pl.ANY),
            pl.BlockSpec((1, 1, f2), lambda i, m: (m[0, i], 0, 0)),
            pl.BlockSpec(memory_space=pl.ANY),
            pl.BlockSpec((1, 1, d), lambda i, m: (m[0, i], 0, 0)),
        ],
        out_specs=pl.BlockSpec((blk * SLAB, LANES), lambda i, m: (i, 0)),
        scratch_shapes=[
            pltpu.VMEM((d, f2), _F32), pltpu.VMEM((f, d), _F32),
            pltpu.VMEM((d, f2), _BF16), pltpu.VMEM((f, d), _BF16),
            pltpu.SemaphoreType.DMA((2,)),
        ],
    )
    return pl.pallas_call(
        _expert_kernel,
        grid_spec=grid_spec,
        out_shape=jax.ShapeDtypeStruct((r * SLAB, LANES), _F32),
        compiler_params=_params("arbitrary"),
        name="experts",
    )(meta, xs, w_gu, b_gu.reshape(ne, 1, f2), w_d, b_d.reshape(ne, 1, d))


def _combine_kernel(alpha, slot_ref, h_ref, gate_ref, g2_ref, b2_ref, ys_ref, o_ref, buf, sem):
    t = h_ref.shape[0]

    def issue(r, carry):
        for k in range(TOP_K):
            _row_copy(ys_ref, slot_ref[0, 0, k * t + r], buf.at[k], r, sem).start(priority=k % 2)
        return carry

    lax.fori_loop(0, t, issue, 0)

    def drain(r8, carry):
        for u in range(DRAIN_UNROLL):
            for k in range(TOP_K):
                _row_copy(ys_ref, 0, buf.at[k], r8 * DRAIN_UNROLL + u, sem).wait()
        return carry

    lax.fori_loop(0, t // DRAIN_UNROLL, drain, 0)

    gate = gate_ref[...]
    moe = gate[:, 0:1] * _load_slabs(buf, t, SLAB, (0,))
    for k in range(1, TOP_K):
        moe = moe + gate[:, k:k + 1] * _load_slabs(buf, t, SLAB, (k,))
    o_ref[...] = _layer_norm(alpha * h_ref[...] + moe, g2_ref[...], b2_ref[...])


def _combine(alpha, slots, h, gate, g2, b2, ys):
    n, d = h.shape
    t = SEQ_TILE
    return pl.pallas_call(
        functools.partial(_combine_kernel, alpha),
        grid=(n // t,),
        in_specs=[
            pl.BlockSpec((1, 1, TOP_K * t), lambda i: (i, 0, 0), memory_space=pltpu.SMEM),
            pl.BlockSpec((t, d), lambda i: (i, 0)),
            pl.BlockSpec((t, LANES), lambda i: (i, 0)),
            pl.BlockSpec((1, d), lambda i: (0, 0)),
            pl.BlockSpec((1, d), lambda i: (0, 0)),
            pl.BlockSpec(memory_space=pl.ANY),
        ],
        out_specs=pl.BlockSpec((t, d), lambda i: (i, 0)),
        out_shape=jax.ShapeDtypeStruct((n, d), _F32),
        scratch_shapes=[pltpu.VMEM((TOP_K, t * SLAB, LANES), _F32), pltpu.SemaphoreType.DMA(())],
        compiler_params=_params("arbitrary"),
        name="combine",
    )(slots, h, gate, g2, b2, ys)


def _pad_cols(a, width):
    return jnp.pad(a, ((0, 0), (0, width - a.shape[1])))


def kernel(x, w_in, b_forget, sink, w_proj_swa, w_proj_fox, w_out, ln1_g, ln1_b, w_router, b_router,
           w_gate_up, b_gate_up, w_down, b_down, ln2_g, ln2_b):
    b, s, d = x.shape
    depth = w_in.shape[0]
    n = b * s
    alpha = float((2.0 * depth) ** 0.25)
    assert s % SEQ_TILE == 0 and SEQ_TILE % WINDOW == 0 and d % LANES == 0
    assert w_router.shape[2] == N_EXPERTS and w_in.shape[2] == ATT_W + FOX_HEADS + 2 * d

    blk = EXPERT_ROWS
    n_rows = ((n * TOP_K + blk - 1) // blk) * blk + N_EXPERTS * blk
    n_blocks_pad = ((n_rows // blk + LANES - 1) // LANES) * LANES
    t = SEQ_TILE
    ri = lax.broadcasted_iota(jnp.int32, (t, t), 0)
    ci = lax.broadcasted_iota(jnp.int32, (t, t), 1)
    tri_incl = (ci <= ri).astype(_BF16)
    tri_strict = (ri < ci).astype(_BF16)

    h = x
    for layer in range(depth):
        w_l = w_in[layer]
        w_att = _pad_cols(w_l[:, :ATT_W + FOX_HEADS], ATT_W + LANES).astype(_BF16)
        w_gates = w_l[:, ATT_W + FOX_HEADS:].astype(_BF16)
        bf_pad = _pad_cols(b_forget[layer].reshape(1, FOX_HEADS).astype(_F32), LANES)

        qkv, ccol, kaug = _in_proj(h, w_att, bf_pad, tri_incl)
        att_a = _swa(qkv, sink[layer].reshape(-1).astype(_F32))
        att_f = _fox(qkv, ccol, kaug)

        h1, h1_slabs, idx_t, gate = _mix(
            alpha, h.reshape(n, d), att_a.reshape(n, SWA_Q_W), att_f.reshape(n, FOX_W),
            w_gates, w_proj_swa[layer].astype(_BF16), w_proj_fox[layer].astype(_BF16),
            w_out[layer].astype(_BF16), ln1_g[layer].reshape(1, d), ln1_b[layer].reshape(1, d),
            _pad_cols(w_router[layer], LANES).astype(_BF16),
            _pad_cols(b_router[layer].reshape(1, N_EXPERTS).astype(_F32), LANES))

        slots, meta = _route(idx_t, tri_strict, n_blocks_pad)
        xs = _dispatch(meta, slots, h1_slabs, n_rows)
        ys = _experts(meta, xs, w_gate_up[layer], b_gate_up[layer], w_down[layer], b_down[layer])
        out = _combine(alpha, slots, h1, gate, ln2_g[layer].reshape(1, d), ln2_b[layer].reshape(1, d), ys)
        h = out.reshape(b, s, d)
    return h
```

```python
import functools

import numpy as np
import jax
import jax.numpy as jnp
from jax import lax
from jax.experimental import pallas as pl
from jax.experimental.pallas import tpu as pltpu

HEAD_DIM = 64
SWA_Q_HEADS = 8
SWA_KV_HEADS = 2
SWA_GROUP = SWA_Q_HEADS // SWA_KV_HEADS
WINDOW = 128
FOX_HEADS = 8
N_EXPERTS = 32
TOP_K = 4
SWIGLU_LIMIT = 7.0
SWIGLU_ALPHA = 1.702
LN_EPS = 1e-5

LANES = 128
SEQ_TILE = 512
EXPERT_ROWS = 512
CAST_ROWS = 64
MIX_SPLIT = 2
FOX_ROW_CHUNK = 64
VMEM_LIMIT = 56 * 1024 * 1024

SWA_Q_W = SWA_Q_HEADS * HEAD_DIM
SWA_KV_W = SWA_KV_HEADS * HEAD_DIM
FOX_W = FOX_HEADS * HEAD_DIM
ATT_W = SWA_Q_W + 2 * SWA_KV_W + 3 * FOX_W
NEG_INF = float("-inf")
LOG2E = 1.4426950408889634
FOX_Q_SCALE = HEAD_DIM ** -0.5 * LOG2E
CQ_POS = 0
CK_POS = 8
CK_PIECES = 3

_F32 = jnp.float32
_BF16 = jnp.bfloat16


def _dot(a, b):
    return jnp.dot(a, b, preferred_element_type=_F32)


def _dot_nt(a, b):
    return lax.dot_general(a, b, (((1,), (1,)), ((), ())), preferred_element_type=_F32)


def _params(*sem):
    return pltpu.CompilerParams(dimension_semantics=sem, vmem_limit_bytes=VMEM_LIMIT)


def _layer_norm(z, g, b):
    mu = jnp.mean(z, axis=-1, keepdims=True)
    zc = z - mu
    var = jnp.mean(zc * zc, axis=-1, keepdims=True)
    return zc * lax.rsqrt(var + LN_EPS) * g + b


def _store_slabs(ref, value, lead=()):
    rows, width = value.shape
    c = width // LANES
    for j in range(c):
        ref[lead + (pl.ds(j, rows, stride=c), slice(None))] = value[:, j * LANES:(j + 1) * LANES]


def _load_slabs(ref, rows, c, lead=(), first=0):
    return jnp.concatenate(
        [ref[lead + (pl.ds(first * c + j, rows, stride=c), slice(None))] for j in range(c)], axis=1)


def _split3(v):
    p1 = v.astype(_BF16).astype(_F32)
    r1 = v - p1
    p2 = r1.astype(_BF16).astype(_F32)
    p3 = (r1 - p2).astype(_BF16).astype(_F32)
    return p1, p2, p3


def _in_proj_kernel(x_ref, w_ref, bf_ref, tri_ref, qkv_ref, ccol_ref, kaug_ref, carry_ref):
    @pl.when(pl.program_id(1) == 0)
    def _():
        carry_ref[...] = jnp.zeros_like(carry_ref)

    xb = x_ref[0].astype(_BF16)
    acc = _dot(xb, w_ref[...])
    qf_col = SWA_Q_W + 2 * SWA_KV_W
    qkv_ref[0, :, :qf_col] = acc[:, :qf_col].astype(_BF16)
    qkv_ref[0, :, qf_col:qf_col + FOX_W] = (acc[:, qf_col:qf_col + FOX_W] * FOX_Q_SCALE).astype(_BF16)
    qkv_ref[0, :, qf_col + FOX_W:] = acc[:, qf_col + FOX_W:ATT_W].astype(_BF16)
    kf_col = SWA_Q_W + 2 * SWA_KV_W + FOX_W
    kt = acc[:, kf_col:kf_col + FOX_W].T
    z = acc[:, ATT_W:] + bf_ref[...]
    log_f = jnp.minimum(z, 0.0) - jnp.log1p(jnp.exp(-jnp.abs(z)))
    tri = tri_ref[...]
    p1 = log_f.astype(_BF16)
    r1 = log_f - p1.astype(_F32)
    p2 = r1.astype(_BF16)
    p3 = (r1 - p2.astype(_F32)).astype(_BF16)
    c = _dot(tri, p1) + _dot(tri, p2) + _dot(tri, p3) + carry_ref[0:1, :]
    t = c.shape[0]
    carry_ref[...] = jnp.broadcast_to(c[t - 1:t, :], carry_ref.shape)
    cl = c * LOG2E
    ccol_ref[0] = cl
    ct = cl.T
    row = lax.broadcasted_iota(jnp.int32, (CK_POS - CQ_POS, t), 0)
    ones_rows = jnp.where(row == 0, 1.0, 0.0)
    pad_rows = jnp.zeros((HEAD_DIM - 2 * (CK_POS - CQ_POS), t), _F32)
    for h in range(FOX_HEADS):
        c1, c2, c3 = _split3(ct[h:h + 1, :])
        ck_rows = jnp.where(row == 0, -c1, jnp.where(row == 1, -c2, jnp.where(row == 2, -c3, 0.0)))
        bias_half = jnp.concatenate([ones_rows, ck_rows, pad_rows], axis=0)
        kh = kt[h * HEAD_DIM:(h + 1) * HEAD_DIM, :]
        halves = [kh, bias_half] if h % 2 == 0 else [bias_half, kh]
        kaug_ref[0, h, 0] = jnp.concatenate(halves, axis=0).astype(_BF16)


def _in_proj(x, w_att, bf_pad, tri):
    b, s, d = x.shape
    t = SEQ_TILE
    wn = w_att.shape[1]
    return pl.pallas_call(
        _in_proj_kernel,
        grid=(b, s // t),
        in_specs=[
            pl.BlockSpec((1, t, d), lambda i, j: (i, j, 0)),
            pl.BlockSpec((d, wn), lambda i, j: (0, 0)),
            pl.BlockSpec((1, LANES), lambda i, j: (0, 0)),
            pl.BlockSpec((t, t), lambda i, j: (0, 0)),
        ],
        out_specs=[
            pl.BlockSpec((1, t, ATT_W), lambda i, j: (i, j, 0)),
            pl.BlockSpec((1, t, LANES), lambda i, j: (i, j, 0)),
            pl.BlockSpec((1, FOX_HEADS, 1, LANES, t), lambda i, j: (i, 0, j, 0, 0)),
        ],
        out_shape=[
            jax.ShapeDtypeStruct((b, s, ATT_W), _BF16),
            jax.ShapeDtypeStruct((b, s, LANES), _F32),
            jax.ShapeDtypeStruct((b, FOX_HEADS, s // t, LANES, t), _BF16),
        ],
        scratch_shapes=[pltpu.VMEM((8, LANES), _F32)],
        compiler_params=_params("arbitrary", "arbitrary"),
        name="in_proj",
    )(x, w_att, bf_pad, tri)


def _alibi_slopes(n):
    return [float(v) for v in np.asarray(2.0 ** (-8.0 * np.arange(1, n + 1) / n), dtype=np.float32)]


def _swa_kernel(sink_ref, q_ref, kp_ref, kc_ref, vp_ref, vc_ref, o_ref):
    blk = pl.program_id(1)
    w = WINDOW
    q = q_ref[0] * jnp.asarray(HEAD_DIM ** -0.5, _BF16)
    k = jnp.concatenate([kp_ref[0], kc_ref[0]], axis=0)
    v = jnp.concatenate([vp_ref[0], vc_ref[0]], axis=0)
    row = lax.broadcasted_iota(jnp.int32, (w, 2 * w), 0)
    col = lax.broadcasted_iota(jnp.int32, (w, 2 * w), 1)
    dist = row + w - col
    valid = (dist >= 0) & (dist < w) & ((col >= w) | (blk > 0))
    distf = dist.astype(_F32)
    slopes = _alibi_slopes(SWA_Q_HEADS)
    outs = []
    for h in range(SWA_Q_HEADS):
        g = h // SWA_GROUP
        qh = q[:, h * HEAD_DIM:(h + 1) * HEAD_DIM]
        kg = k[:, g * HEAD_DIM:(g + 1) * HEAD_DIM]
        vg = v[:, g * HEAD_DIM:(g + 1) * HEAD_DIM]
        sc = _dot_nt(qh, kg) - slopes[h] * distf
        sc = jnp.where(valid, sc, NEG_INF)
        sk = sink_ref[h]
        m = jnp.maximum(jnp.max(sc, axis=1, keepdims=True), sk)
        p = jnp.exp(sc - m)
        denom = jnp.sum(p, axis=1, keepdims=True) + jnp.exp(sk - m)
        o = _dot(p.astype(_BF16), vg)
        outs.append(o / denom)
    o_ref[0] = jnp.concatenate(outs, axis=1).astype(_BF16)


def _swa(qkv, sink_flat):
    b, s, _ = qkv.shape
    w = WINDOW
    kcol = SWA_Q_W // LANES
    vcol = kcol + SWA_KV_W // LANES
    grid_spec = pltpu.PrefetchScalarGridSpec(
        num_scalar_prefetch=1,
        grid=(b, s // w),
        in_specs=[
            pl.BlockSpec((1, w, SWA_Q_W), lambda i, j, sk: (i, j, 0)),
            pl.BlockSpec((1, w, SWA_KV_W), lambda i, j, sk: (i, jnp.maximum(j - 1, 0), kcol)),
            pl.BlockSpec((1, w, SWA_KV_W), lambda i, j, sk: (i, j, kcol)),
            pl.BlockSpec((1, w, SWA_KV_W), lambda i, j, sk: (i, jnp.maximum(j - 1, 0), vcol)),
            pl.BlockSpec((1, w, SWA_KV_W), lambda i, j, sk: (i, j, vcol)),
        ],
        out_specs=pl.BlockSpec((1, w, SWA_Q_W), lambda i, j, sk: (i, j, 0)),
    )
    return pl.pallas_call(
        _swa_kernel,
        grid_spec=grid_spec,
        out_shape=jax.ShapeDtypeStruct((b, s, SWA_Q_W), _BF16),
        compiler_params=_params("arbitrary", "arbitrary"),
        name="swa",
    )(sink_flat, qkv, qkv, qkv, qkv, qkv)


def _fox_kernel(q_ref, kaug_ref, v_ref, ccol_ref, o_ref, m_ref, acc_ref, s_ref, p_ref):
    pair = pl.program_id(1)
    t = SEQ_TILE
    nq = q_ref.shape[1] // t
    lane = lax.broadcasted_iota(jnp.int32, (t, LANES), 1)
    low = lane < HEAD_DIM
    rc = FOX_ROW_CHUNK
    low_c = lax.broadcasted_iota(jnp.int32, (rc, LANES), 1) < HEAD_DIM
    ones0 = jnp.where(low, 1.0, 0.0).astype(_BF16)
    ones1 = jnp.where(low, 0.0, 1.0).astype(_BF16)

    def q_aug(qi):
        q = q_ref[0, qi * t:(qi + 1) * t, :].astype(_F32)
        ccol = ccol_ref[0, qi * t:(qi + 1) * t, :]
        qa = []
        for hh in range(2):
            cq = jnp.sum(jnp.where(lane == 2 * pair + hh, ccol, 0.0), axis=1, keepdims=True)
            base = HEAD_DIM if hh == 0 else 0
            bias = jnp.where(lane == base + CQ_POS, cq, 0.0)
            bias = jnp.where((lane >= base + CK_POS) & (lane < base + CK_POS + CK_PIECES), 1.0, bias)
            qa.append(jnp.where(low if hh == 0 else ~low, q, bias).astype(_BF16))
        return qa

    def logits(slot, qa, j):
        for hh in range(2):
            s_ref[slot, hh] = _dot(qa[hh], kaug_ref[0, hh, j])

    def absorb(slot, j, masked):
        v = v_ref[0, j * t:(j + 1) * t, :]
        zero = jnp.zeros_like(v)
        vcat = jnp.concatenate([jnp.concatenate([jnp.where(low, v, zero), ones0], axis=1),
                                jnp.concatenate([jnp.where(low, zero, v), ones1], axis=1)], axis=0)
        for r in range(t // rc):
            rows = pl.ds(r * rc, rc)
            if masked:
                keep = (lax.broadcasted_iota(jnp.int32, (rc, t), 1)
                        <= lax.broadcasted_iota(jnp.int32, (rc, t), 0) + r * rc)
            alphas = []
            for hh in range(2):
                s = s_ref[slot, hh, rows, :]
                if masked:
                    s = jnp.where(keep, s, NEG_INF)
                m_old = m_ref[hh, rows, :]
                m_new = jnp.maximum(m_old, jnp.max(s, axis=1, keepdims=True))
                alpha = jnp.exp2(m_old - m_new)
                p = jnp.exp2(s - jnp.tile(m_new, (1, t // LANES)))
                m_ref[hh, rows, :] = m_new
                p_ref[rows, hh * t:(hh + 1) * t] = p.astype(_BF16)
                alphas.append(alpha)
            scale = jnp.where(low_c, alphas[0], alphas[1])
            acc_ref[rows, :] = jnp.concatenate([scale, scale], axis=1) * acc_ref[rows, :]
        acc_ref[...] = acc_ref[...] + _dot(p_ref[...], vcat)

    visits = [(qi, j) for qi in range(nq) for j in range(qi + 1)]
    qa = q_aug(0)
    logits(0, qa, 0)
    for u, (qi, j) in enumerate(visits):
        slot = u % 2
        if j == 0:
            m_ref[...] = jnp.full_like(m_ref, NEG_INF)
            acc_ref[...] = jnp.zeros_like(acc_ref)
        if u + 1 < len(visits):
            nqi, nj = visits[u + 1]
            if nj == 0:
                qa = q_aug(nqi)
            logits(1 - slot, qa, nj)
        absorb(slot, j, j == qi)
        if j == qi:
            acc = acc_ref[...]
            o_ref[0, qi * t:(qi + 1) * t, :] = (acc[:, :LANES] / acc[:, LANES:]).astype(_BF16)


def _fox(qkv, ccol, kaug):
    b, s, _ = qkv.shape
    t = SEQ_TILE
    pairs = FOX_HEADS // 2
    qcol = (SWA_Q_W + 2 * SWA_KV_W) // LANES
    vcol = qcol + 2 * FOX_W // LANES
    return pl.pallas_call(
        _fox_kernel,
        grid=(b, pairs),
        in_specs=[
            pl.BlockSpec((1, s, LANES), lambda i, p: (i, 0, qcol + p)),
            pl.BlockSpec((1, 2, s // t, LANES, t), lambda i, p: (i, p, 0, 0, 0)),
            pl.BlockSpec((1, s, LANES), lambda i, p: (i, 0, vcol + p)),
            pl.BlockSpec((1, s, LANES), lambda i, p: (i, 0, 0)),
        ],
        out_specs=pl.BlockSpec((1, s, LANES), lambda i, p: (i, 0, p)),
        out_shape=jax.ShapeDtypeStruct((b, s, FOX_W), _BF16),
        scratch_shapes=[
            pltpu.VMEM((2, t, LANES), _F32),
            pltpu.VMEM((t, 2 * LANES), _F32),
            pltpu.VMEM((2, 2, t, t), _F32),
            pltpu.VMEM((t, 2 * t), _BF16),
        ],
        compiler_params=_params("arbitrary", "arbitrary"),
        name="fox",
    )(qkv, kaug, qkv, ccol)


def _mix_kernel(alpha, x_ref, aa_ref, af_ref, wg_ref, wps_ref, wpf_ref, wo_ref, g1_ref, b1_ref,
                wr_ref, br_ref, h_ref, hs_ref, idx_ref, gate_ref):
    d = x_ref.shape[1]
    t = x_ref.shape[0] // MIX_SPLIT
    for part in range(MIX_SPLIT):
        rows = slice(part * t, (part + 1) * t)
        x = x_ref[rows, :]
        gates = _dot(x.astype(_BF16), wg_ref[...])
        ya = _dot(aa_ref[rows, :], wps_ref[...])
        yf = _dot(af_ref[rows, :], wpf_ref[...])
        mix = jax.nn.sigmoid(gates[:, :d]) * ya + jax.nn.sigmoid(gates[:, d:]) * yf
        z = alpha * x + _dot(mix.astype(_BF16), wo_ref[...])
        h = _layer_norm(z, g1_ref[...], b1_ref[...])
        h_ref[rows, :] = h
        c = d // LANES
        for j in range(c):
            hs_ref[pl.ds(part * t * c + j, t, stride=c), :] = h[:, j * LANES:(j + 1) * LANES]

        logits = _dot(h.astype(_BF16), wr_ref[...]) + br_ref[...]
        lt = logits.T[:N_EXPERTS, :]
        eidx = lax.broadcasted_iota(jnp.int32, (N_EXPERTS, t), 0)
        work = lt
        vals, idxs = [], []
        for _ in range(TOP_K):
            mk = jnp.max(work, axis=0, keepdims=True)
            ik = jnp.min(jnp.where(work == mk, eidx, N_EXPERTS), axis=0, keepdims=True)
            vals.append(mk)
            idxs.append(ik)
            work = jnp.where(eidx == ik, NEG_INF, work)
        ex = [jnp.exp(vk - vals[0]) for vk in vals]
        denom = ex[0] + ex[1] + ex[2] + ex[3]
        idx_ref[:, rows] = jnp.concatenate(idxs + [jnp.zeros((8 - TOP_K, t), jnp.int32)], axis=0)
        gt = jnp.concatenate([e / denom for e in ex] + [jnp.zeros((LANES - TOP_K, t), _F32)], axis=0)
        gate_ref[rows, :] = gt.T


def _mix(alpha, x2, att_a, att_f, wg, wps, wpf, wo, g1, b1, wr, br):
    n, d = x2.shape
    t = SEQ_TILE
    const = lambda shape: pl.BlockSpec(shape, lambda i: (0, 0))
    return pl.pallas_call(
        functools.partial(_mix_kernel, alpha),
        grid=(n // t,),
        in_specs=[
            pl.BlockSpec((t, d), lambda i: (i, 0)),
            pl.BlockSpec((t, SWA_Q_W), lambda i: (i, 0)),
            pl.BlockSpec((t, FOX_W), lambda i: (i, 0)),
            const(wg.shape), const(wps.shape), const(wpf.shape), const(wo.shape),
            const(g1.shape), const(b1.shape), const(wr.shape), const(br.shape),
        ],
        out_specs=[
            pl.BlockSpec((t, d), lambda i: (i, 0)),
            pl.BlockSpec((t * d // LANES, LANES), lambda i: (i, 0)),
            pl.BlockSpec((8, t), lambda i: (0, i)),
            pl.BlockSpec((t, LANES), lambda i: (i, 0)),
        ],
        out_shape=[
            jax.ShapeDtypeStruct((n, d), _F32),
            jax.ShapeDtypeStruct((n * d // LANES, LANES), _F32),
            jax.ShapeDtypeStruct((8, n), jnp.int32),
            jax.ShapeDtypeStruct((n, LANES), _F32),
        ],
        compiler_params=_params("arbitrary"),
        name="mix",
    )(x2, att_a, att_f, wg, wps, wpf, wo, g1, b1, wr, br)


def _route_kernel(idx_ref, tri_ref, slot_ref, meta_ref, cnt_ref, run_ref):
    phase = pl.program_id(0)
    i = pl.program_id(1)
    t = idx_ref.shape[1]
    blk = EXPERT_ROWS
    eidx = lax.broadcasted_iota(jnp.int32, (N_EXPERTS, t), 0)
    idx = idx_ref[...]
    sel = (eidx == idx[0:1, :])
    for k in range(1, TOP_K):
        sel = sel | (eidx == idx[k:k + 1, :])
    self32 = jnp.where(sel, 1.0, 0.0)
    tile_cnt = jnp.sum(self32, axis=1, keepdims=True)

    @pl.when((phase == 0) & (i == 0))
    def _():
        cnt_ref[...] = jnp.zeros_like(cnt_ref)

    @pl.when(phase == 0)
    def _():
        cnt_ref[...] = cnt_ref[...] + tile_cnt

    @pl.when((phase == 1) & (i == 0))
    def _():
        cnt = cnt_ref[...]
        padded = jnp.floor((cnt + (blk - 1)) * (1.0 / blk)) * blk
        run = jnp.zeros((1, LANES), _F32)
        rows = []
        for e in range(N_EXPERTS):
            rows.append(run)
            run = run + padded[e:e + 1, :]
        start = jnp.concatenate(rows, axis=0)
        run_ref[...] = start
        nbp = meta_ref.shape[1]
        s1 = start[:, 0:1]
        e1 = s1 + padded[:, 0:1]
        c1 = cnt[:, 0:1]
        row0 = (lax.broadcasted_iota(jnp.int32, (N_EXPERTS, nbp), 1) * blk).astype(_F32)
        owner = jnp.sum(jnp.where(e1 <= row0, 1.0, 0.0), axis=0, keepdims=True)
        owner = jnp.minimum(owner, N_EXPERTS - 1.0)
        inside = (s1 <= row0) & (row0 < e1)
        nval = jnp.sum(jnp.where(inside, jnp.clip(c1 - (row0 - s1), 0.0, blk), 0.0), axis=0, keepdims=True)
        erow = lax.broadcasted_iota(jnp.int32, (N_EXPERTS, nbp), 0).astype(_F32)
        later = (erow > owner) & (c1 > 0.0)
        nxt = jnp.min(jnp.where(later, erow, float(N_EXPERTS)), axis=0, keepdims=True)
        meta_ref[...] = jnp.concatenate(
            [owner.astype(jnp.int32), nval.astype(jnp.int32), nxt.astype(jnp.int32),
             jnp.zeros((5, nbp), jnp.int32)], axis=0)

    @pl.when(phase == 1)
    def _():
        rank = _dot(self32.astype(_BF16), tri_ref[...])
        slot = (run_ref[:, 0:1] + rank).astype(jnp.int32)
        for k in range(TOP_K):
            sk = jnp.sum(jnp.where(eidx == idx[k:k + 1, :], slot, 0), axis=0, keepdims=True)
            slot_ref[0, :, k * t:(k + 1) * t] = sk
        run_ref[...] = run_ref[...] + tile_cnt


def _route(idx_t, tri_strict, n_blocks_pad):
    n = idx_t.shape[1]
    t = SEQ_TILE
    nt = n // t
    return pl.pallas_call(
        _route_kernel,
        grid=(2, nt),
        in_specs=[
            pl.BlockSpec((8, t), lambda ph, i: (0, i)),
            pl.BlockSpec((t, t), lambda ph, i: (0, 0)),
        ],
        out_specs=[
            pl.BlockSpec((1, 1, TOP_K * t), lambda ph, i: (ph * i, 0, 0)),
            pl.BlockSpec((8, n_blocks_pad), lambda ph, i: (0, 0)),
        ],
        out_shape=[
            jax.ShapeDtypeStruct((nt, 1, TOP_K * t), jnp.int32),
            jax.ShapeDtypeStruct((8, n_blocks_pad), jnp.int32),
        ],
        scratch_shapes=[
            pltpu.VMEM((N_EXPERTS, LANES), _F32),
            pltpu.VMEM((N_EXPERTS, LANES), _F32),
        ],
        compiler_params=_params("arbitrary", "arbitrary"),
        name="route",
    )(idx_t, tri_strict)


SLAB = 8
DRAIN_UNROLL = 8
COMBINE_PARTS = 2


def _row_copy(src_ref, src_row, dst_ref, dst_row, sem):
    src = src_ref.at[pl.ds(pl.multiple_of(src_row * SLAB, SLAB), SLAB), :]
    dst = dst_ref.at[pl.ds(pl.multiple_of(dst_row * SLAB, SLAB), SLAB), :]
    return pltpu.make_async_copy(src, dst, sem)


def _dispatch_kernel(meta_ref, slot_ref, h_ref, xs_ref, zero_buf, sem, zsem):
    t = h_ref.shape[0] // SLAB
    blk = zero_buf.shape[0]

    @pl.when(pl.program_id(0) == 0)
    def _():
        zero_buf[...] = jnp.zeros_like(zero_buf)
        n_blocks = xs_ref.shape[0] // blk

        def fill(bi):
            return pltpu.make_async_copy(zero_buf, xs_ref.at[pl.ds(pl.multiple_of(bi * blk, blk), blk), :], zsem)

        def start(bi, carry):
            @pl.when(meta_ref[1, bi] < EXPERT_ROWS)
            def _():
                fill(bi).start()
            return carry

        def wait(bi, carry):
            @pl.when(meta_ref[1, bi] < EXPERT_ROWS)
            def _():
                fill(bi).wait()
            return carry

        lax.fori_loop(0, n_blocks, start, 0)
        lax.fori_loop(0, n_blocks, wait, 0)

    def issue(r, carry):
        for k in range(TOP_K):
            _row_copy(h_ref, r, xs_ref, slot_ref[0, 0, k * t + r], sem).start(priority=k % 2)
        return carry

    lax.fori_loop(0, t, issue, 0)

    def drain(r8, carry):
        for u in range(DRAIN_UNROLL):
            for k in range(TOP_K):
                _row_copy(h_ref, r8 * DRAIN_UNROLL + u, xs_ref, 0, sem).wait()
        return carry

    lax.fori_loop(0, t // DRAIN_UNROLL, drain, 0)


def _dispatch(meta, slots, hs, n_rows):
    n = hs.shape[0] // SLAB
    t = SEQ_TILE
    grid_spec = pltpu.PrefetchScalarGridSpec(
        num_scalar_prefetch=1,
        grid=(n // t,),
        in_specs=[
            pl.BlockSpec((1, 1, TOP_K * t), lambda i, m: (i, 0, 0), memory_space=pltpu.SMEM),
            pl.BlockSpec((t * SLAB, LANES), lambda i, m: (i, 0)),
        ],
        out_specs=pl.BlockSpec(memory_space=pl.ANY),
        scratch_shapes=[pltpu.VMEM((EXPERT_ROWS * SLAB, LANES), _F32), pltpu.SemaphoreType.DMA(()),
                        pltpu.SemaphoreType.DMA(())],
    )
    return pl.pallas_call(
        _dispatch_kernel,
        grid_spec=grid_spec,
        out_shape=jax.ShapeDtypeStruct((n_rows * SLAB, LANES), _F32),
        compiler_params=_params("arbitrary"),
        name="dispatch",
    )(meta, slots, hs)


def _expert_kernel(meta_ref, x_ref, wgu_hbm, bgu_ref, wd_hbm, bd_ref, y_ref,
                   wgu_land, wd_land, wgu_bf, wd_bf, sems):
    b = pl.program_id(0)
    e = meta_ref[0, b]
    nval = meta_ref[1, b]
    prev = meta_ref[0, jnp.maximum(b - 1, 0)]
    f = wd_bf.shape[0]

    def fetch(expert):
        return (pltpu.make_async_copy(wgu_hbm.at[expert], wgu_land, sems.at[0]),
                pltpu.make_async_copy(wd_hbm.at[expert], wd_land, sems.at[1]))

    @pl.when(b == 0)
    def _():
        for cp in fetch(e):
            cp.start()

    @pl.when((nval > 0) & ((b == 0) | (prev != e)))
    def _():
        for cp in fetch(e):
            cp.wait()

        def convert(i, carry):
            rows = pl.ds(pl.multiple_of(i * CAST_ROWS, CAST_ROWS), CAST_ROWS)
            wgu_bf[rows, :] = wgu_land[rows, :].astype(_BF16)
            wd_bf[rows, :] = wd_land[rows, :].astype(_BF16)
            return carry

        lax.fori_loop(0, wgu_land.shape[0] // CAST_ROWS, convert, 0)
        nxt = meta_ref[2, b]

        @pl.when(nxt < N_EXPERTS)
        def _():
            for cp in fetch(nxt):
                cp.start(priority=1)

    def swiglu(rows):
        x = _load_slabs(x_ref, rows, SLAB).astype(_BF16)
        hgu = _dot(x, wgu_bf[...]) + bgu_ref[0]
        gate = jnp.minimum(hgu[:, :f], SWIGLU_LIMIT)
        up = jnp.clip(hgu[:, f:], -SWIGLU_LIMIT, SWIGLU_LIMIT)
        act = gate * jax.nn.sigmoid(gate * SWIGLU_ALPHA) * (up + 1.0)
        _store_slabs(y_ref, _dot(act.astype(_BF16), wd_bf[...]) + bd_ref[0])

    half = EXPERT_ROWS // 2

    @pl.when(nval > half)
    def _():
        swiglu(EXPERT_ROWS)

    @pl.when((nval > 0) & (nval <= half))
    def _():
        swiglu(half)
        y_ref[half * SLAB:, :] = jnp.zeros((half * SLAB, LANES), _F32)

    @pl.when(nval <= 0)
    def _():
        y_ref[...] = jnp.zeros_like(y_ref)


def _experts(meta, xs, w_gu, b_gu, w_d, b_d):
    ne, d, f2 = w_gu.shape
    assert d == SLAB * LANES
    r = xs.shape[0] // SLAB
    f = w_d.shape[1]
    blk = EXPERT_ROWS
    grid_spec = pltpu.PrefetchScalarGridSpec(
        num_scalar_prefetch=1,
        grid=(r // blk,),
        in_specs=[
            pl.BlockSpec((blk * SLAB, LANES), lambda i, m: (i, 0)),
            pl.BlockSpec(memory_space=pl.ANY),
            pl.BlockSpec((1, 1, f2), lambda i, m: (m[0, i], 0, 0)),
            pl.BlockSpec(memory_space=pl.ANY),
            pl.BlockSpec((1, 1, d), lambda i, m: (m[0, i], 0, 0)),
        ],
        out_specs=pl.BlockSpec((blk * SLAB, LANES), lambda i, m: (i, 0)),
        scratch_shapes=[
            pltpu.VMEM((d, f2), _F32), pltpu.VMEM((f, d), _F32),
            pltpu.VMEM((d, f2), _BF16), pltpu.VMEM((f, d), _BF16),
            pltpu.SemaphoreType.DMA((2,)),
        ],
    )
    return pl.pallas_call(
        _expert_kernel,
        grid_spec=grid_spec,
        out_shape=jax.ShapeDtypeStruct((r * SLAB, LANES), _F32),
        compiler_params=_params("arbitrary"),
        name="experts",
    )(meta, xs, w_gu, b_gu.reshape(ne, 1, f2), w_d, b_d.reshape(ne, 1, d))


def _combine_kernel(alpha, slot_ref, h_ref, gate_ref, g2_ref, b2_ref, ys_ref, o_ref, buf, sems):
    t = h_ref.shape[0]
    tp = t // COMBINE_PARTS

    def issue(part):
        def body(r, carry):
            tok = part * tp + r
            for k in range(TOP_K):
                _row_copy(ys_ref, slot_ref[0, 0, k * t + tok], buf.at[k], tok, sems.at[part]).start(priority=k % 2)
            return carry

        lax.fori_loop(0, tp, body, 0)

    def drain(part):
        def body(r8, carry):
            for u in range(DRAIN_UNROLL):
                for k in range(TOP_K):
                    _row_copy(ys_ref, 0, buf.at[k], part * tp + r8 * DRAIN_UNROLL + u, sems.at[part]).wait()
            return carry

        lax.fori_loop(0, tp // DRAIN_UNROLL, body, 0)

    def finish(part):
        rows = slice(part * tp, (part + 1) * tp)
        gate = gate_ref[rows, :]
        moe = gate[:, 0:1] * _load_slabs(buf, tp, SLAB, (0,), part * tp)
        for k in range(1, TOP_K):
            moe = moe + gate[:, k:k + 1] * _load_slabs(buf, tp, SLAB, (k,), part * tp)
        o_ref[rows, :] = _layer_norm(alpha * h_ref[rows, :] + moe, g2_ref[...], b2_ref[...])

    for part in range(COMBINE_PARTS):
        issue(part)
    for part in range(COMBINE_PARTS):
        drain(part)
        finish(part)


def _combine(alpha, slots, h, gate, g2, b2, ys):
    n, d = h.shape
    t = SEQ_TILE
    return pl.pallas_call(
        functools.partial(_combine_kernel, alpha),
        grid=(n // t,),
        in_specs=[
            pl.BlockSpec((1, 1, TOP_K * t), lambda i: (i, 0, 0), memory_space=pltpu.SMEM),
            pl.BlockSpec((t, d), lambda i: (i, 0)),
            pl.BlockSpec((t, LANES), lambda i: (i, 0)),
            pl.BlockSpec((1, d), lambda i: (0, 0)),
            pl.BlockSpec((1, d), lambda i: (0, 0)),
            pl.BlockSpec(memory_space=pl.ANY),
        ],
        out_specs=pl.BlockSpec((t, d), lambda i: (i, 0)),
        out_shape=jax.ShapeDtypeStruct((n, d), _F32),
        scratch_shapes=[pltpu.VMEM((TOP_K, t * SLAB, LANES), _F32), pltpu.SemaphoreType.DMA((COMBINE_PARTS,))],
        compiler_params=_params("arbitrary"),
        name="combine",
    )(slots, h, gate, g2, b2, ys)


def _pad_cols(a, width):
    return jnp.pad(a, ((0, 0), (0, width - a.shape[1])))


def kernel(x, w_in, b_forget, sink, w_proj_swa, w_proj_fox, w_out, ln1_g, ln1_b, w_router, b_router,
           w_gate_up, b_gate_up, w_down, b_down, ln2_g, ln2_b):
    b, s, d = x.shape
    depth = w_in.shape[0]
    n = b * s
    alpha = float((2.0 * depth) ** 0.25)
    assert s % SEQ_TILE == 0 and SEQ_TILE % WINDOW == 0 and d % LANES == 0
    assert w_router.shape[2] == N_EXPERTS and w_in.shape[2] == ATT_W + FOX_HEADS + 2 * d

    blk = EXPERT_ROWS
    n_rows = ((n * TOP_K + blk - 1) // blk) * blk + N_EXPERTS * blk
    n_blocks_pad = ((n_rows // blk + LANES - 1) // LANES) * LANES
    t = SEQ_TILE
    ri = lax.broadcasted_iota(jnp.int32, (t, t), 0)
    ci = lax.broadcasted_iota(jnp.int32, (t, t), 1)
    tri_incl = (ci <= ri).astype(_BF16)
    tri_strict = (ri < ci).astype(_BF16)

    h = x
    for layer in range(depth):
        w_l = w_in[layer]
        w_att = _pad_cols(w_l[:, :ATT_W + FOX_HEADS], ATT_W + LANES).astype(_BF16)
        w_gates = w_l[:, ATT_W + FOX_HEADS:].astype(_BF16)
        bf_pad = _pad_cols(b_forget[layer].reshape(1, FOX_HEADS).astype(_F32), LANES)

        qkv, ccol, kaug = _in_proj(h, w_att, bf_pad, tri_incl)
        att_a = _swa(qkv, sink[layer].reshape(-1).astype(_F32))
        att_f = _fox(qkv, ccol, kaug)

        h1, h1_slabs, idx_t, gate = _mix(
            alpha, h.reshape(n, d), att_a.reshape(n, SWA_Q_W), att_f.reshape(n, FOX_W),
            w_gates, w_proj_swa[layer].astype(_BF16), w_proj_fox[layer].astype(_BF16),
            w_out[layer].astype(_BF16), ln1_g[layer].reshape(1, d), ln1_b[layer].reshape(1, d),
            _pad_cols(w_router[layer], LANES).astype(_BF16),
            _pad_cols(b_router[layer].reshape(1, N_EXPERTS).astype(_F32), LANES))

        slots, meta = _route(idx_t, tri_strict, n_blocks_pad)
        xs = _dispatch(meta, slots, h1_slabs, n_rows)
        ys = _experts(meta, xs, w_gate_up[layer], b_gate_up[layer], w_down[layer], b_down[layer])
        out = _combine(alpha, slots, h1, gate, ln2_g[layer].reshape(1, d), ln2_b[layer].reshape(1, d), ys)
        h = out.reshape(b, s, d)
    return h
```

```python
import functools

import numpy as np
import jax
import jax.numpy as jnp
from jax import lax
from jax.experimental import pallas as pl
from jax.experimental.pallas import tpu as pltpu

HEAD_DIM = 64
SWA_Q_HEADS = 8
SWA_KV_HEADS = 2
SWA_GROUP = SWA_Q_HEADS // SWA_KV_HEADS
WINDOW = 128
FOX_HEADS = 8
N_EXPERTS = 32
TOP_K = 4
SWIGLU_LIMIT = 7.0
SWIGLU_ALPHA = 1.702
LN_EPS = 1e-5

LANES = 128
SEQ_TILE = 512
EXPERT_ROWS = 512
CAST_ROWS = 64
MIX_SPLIT = 2
FOX_ROW_CHUNK = 64
VMEM_LIMIT = 56 * 1024 * 1024

SWA_Q_W = SWA_Q_HEADS * HEAD_DIM
SWA_KV_W = SWA_KV_HEADS * HEAD_DIM
FOX_W = FOX_HEADS * HEAD_DIM
ATT_W = SWA_Q_W + 2 * SWA_KV_W + 3 * FOX_W
NEG_INF = float("-inf")
LOG2E = 1.4426950408889634
FOX_Q_SCALE = HEAD_DIM ** -0.5 * LOG2E
CQ_POS = 0
CK_POS = 8
CK_PIECES = 3

_F32 = jnp.float32
_BF16 = jnp.bfloat16


def _dot(a, b):
    return jnp.dot(a, b, preferred_element_type=_F32)


def _dot_nt(a, b):
    return lax.dot_general(a, b, (((1,), (1,)), ((), ())), preferred_element_type=_F32)


def _params(*sem):
    return pltpu.CompilerParams(dimension_semantics=sem, vmem_limit_bytes=VMEM_LIMIT)


def _layer_norm(z, g, b):
    mu = jnp.mean(z, axis=-1, keepdims=True)
    zc = z - mu
    var = jnp.mean(zc * zc, axis=-1, keepdims=True)
    return zc * lax.rsqrt(var + LN_EPS) * g + b


def _store_slabs(ref, value, lead=()):
    rows, width = value.shape
    c = width // LANES
    for j in range(c):
        ref[lead + (pl.ds(j, rows, stride=c), slice(None))] = value[:, j * LANES:(j + 1) * LANES]


def _load_slabs(ref, rows, c, lead=(), first=0):
    return jnp.concatenate(
        [ref[lead + (pl.ds(first * c + j, rows, stride=c), slice(None))] for j in range(c)], axis=1)


def _split3(v):
    p1 = v.astype(_BF16).astype(_F32)
    r1 = v - p1
    p2 = r1.astype(_BF16).astype(_F32)
    p3 = (r1 - p2).astype(_BF16).astype(_F32)
    return p1, p2, p3


def _in_proj_kernel(x_ref, w_ref, bf_ref, tri_ref, qkv_ref, ccol_ref, kaug_ref, carry_ref):
    @pl.when(pl.program_id(1) == 0)
    def _():
        carry_ref[...] = jnp.zeros_like(carry_ref)

    xb = x_ref[0].astype(_BF16)
    acc = _dot(xb, w_ref[...])
    qf_col = SWA_Q_W + 2 * SWA_KV_W
    qkv_ref[0, :, :qf_col] = acc[:, :qf_col].astype(_BF16)
    qkv_ref[0, :, qf_col:qf_col + FOX_W] = (acc[:, qf_col:qf_col + FOX_W] * FOX_Q_SCALE).astype(_BF16)
    qkv_ref[0, :, qf_col + FOX_W:] = acc[:, qf_col + FOX_W:ATT_W].astype(_BF16)
    kf_col = SWA_Q_W + 2 * SWA_KV_W + FOX_W
    kt = acc[:, kf_col:kf_col + FOX_W].T
    z = acc[:, ATT_W:] + bf_ref[...]
    log_f = jnp.minimum(z, 0.0) - jnp.log1p(jnp.exp(-jnp.abs(z)))
    tri = tri_ref[...]
    p1 = log_f.astype(_BF16)
    r1 = log_f - p1.astype(_F32)
    p2 = r1.astype(_BF16)
    p3 = (r1 - p2.astype(_F32)).astype(_BF16)
    c = _dot(tri, p1) + _dot(tri, p2) + _dot(tri, p3) + carry_ref[0:1, :]
    t = c.shape[0]
    carry_ref[...] = jnp.broadcast_to(c[t - 1:t, :], carry_ref.shape)
    cl = c * LOG2E
    ccol_ref[0] = cl
    ct = cl.T
    row = lax.broadcasted_iota(jnp.int32, (CK_POS - CQ_POS, t), 0)
    ones_rows = jnp.where(row == 0, 1.0, 0.0)
    pad_rows = jnp.zeros((HEAD_DIM - 2 * (CK_POS - CQ_POS), t), _F32)
    for h in range(FOX_HEADS):
        c1, c2, c3 = _split3(ct[h:h + 1, :])
        ck_rows = jnp.where(row == 0, -c1, jnp.where(row == 1, -c2, jnp.where(row == 2, -c3, 0.0)))
        bias_half = jnp.concatenate([ones_rows, ck_rows, pad_rows], axis=0)
        kh = kt[h * HEAD_DIM:(h + 1) * HEAD_DIM, :]
        halves = [kh, bias_half] if h % 2 == 0 else [bias_half, kh]
        kaug_ref[0, h, 0] = jnp.concatenate(halves, axis=0).astype(_BF16)


def _in_proj(x, w_att, bf_pad, tri):
    b, s, d = x.shape
    t = SEQ_TILE
    wn = w_att.shape[1]
    return pl.pallas_call(
        _in_proj_kernel,
        grid=(b, s // t),
        in_specs=[
            pl.BlockSpec((1, t, d), lambda i, j: (i, j, 0)),
            pl.BlockSpec((d, wn), lambda i, j: (0, 0)),
            pl.BlockSpec((1, LANES), lambda i, j: (0, 0)),
            pl.BlockSpec((t, t), lambda i, j: (0, 0)),
        ],
        out_specs=[
            pl.BlockSpec((1, t, ATT_W), lambda i, j: (i, j, 0)),
            pl.BlockSpec((1, t, LANES), lambda i, j: (i, j, 0)),
            pl.BlockSpec((1, FOX_HEADS, 1, LANES, t), lambda i, j: (i, 0, j, 0, 0)),
        ],
        out_shape=[
            jax.ShapeDtypeStruct((b, s, ATT_W), _BF16),
            jax.ShapeDtypeStruct((b, s, LANES), _F32),
            jax.ShapeDtypeStruct((b, FOX_HEADS, s // t, LANES, t), _BF16),
        ],
        scratch_shapes=[pltpu.VMEM((8, LANES), _F32)],
        compiler_params=_params("arbitrary", "arbitrary"),
        name="in_proj",
    )(x, w_att, bf_pad, tri)


def _alibi_slopes(n):
    return [float(v) for v in np.asarray(2.0 ** (-8.0 * np.arange(1, n + 1) / n), dtype=np.float32)]


def _swa_kernel(sink_ref, q_ref, kp_ref, kc_ref, vp_ref, vc_ref, o_ref):
    blk = pl.program_id(1)
    w = WINDOW
    q = q_ref[0] * jnp.asarray(HEAD_DIM ** -0.5, _BF16)
    k = jnp.concatenate([kp_ref[0], kc_ref[0]], axis=0)
    v = jnp.concatenate([vp_ref[0], vc_ref[0]], axis=0)
    row = lax.broadcasted_iota(jnp.int32, (w, 2 * w), 0)
    col = lax.broadcasted_iota(jnp.int32, (w, 2 * w), 1)
    dist = row + w - col
    valid = (dist >= 0) & (dist < w) & ((col >= w) | (blk > 0))
    distf = dist.astype(_F32)
    slopes = _alibi_slopes(SWA_Q_HEADS)
    outs = []
    for h in range(SWA_Q_HEADS):
        g = h // SWA_GROUP
        qh = q[:, h * HEAD_DIM:(h + 1) * HEAD_DIM]
        kg = k[:, g * HEAD_DIM:(g + 1) * HEAD_DIM]
        vg = v[:, g * HEAD_DIM:(g + 1) * HEAD_DIM]
        sc = _dot_nt(qh, kg) - slopes[h] * distf
        sc = jnp.where(valid, sc, NEG_INF)
        sk = sink_ref[h]
        m = jnp.maximum(jnp.max(sc, axis=1, keepdims=True), sk)
        p = jnp.exp(sc - m)
        denom = jnp.sum(p, axis=1, keepdims=True) + jnp.exp(sk - m)
        o = _dot(p.astype(_BF16), vg)
        outs.append(o / denom)
    o_ref[0] = jnp.concatenate(outs, axis=1).astype(_BF16)


def _swa(qkv, sink_flat):
    b, s, _ = qkv.shape
    w = WINDOW
    kcol = SWA_Q_W // LANES
    vcol = kcol + SWA_KV_W // LANES
    grid_spec = pltpu.PrefetchScalarGridSpec(
        num_scalar_prefetch=1,
        grid=(b, s // w),
        in_specs=[
            pl.BlockSpec((1, w, SWA_Q_W), lambda i, j, sk: (i, j, 0)),
            pl.BlockSpec((1, w, SWA_KV_W), lambda i, j, sk: (i, jnp.maximum(j - 1, 0), kcol)),
            pl.BlockSpec((1, w, SWA_KV_W), lambda i, j, sk: (i, j, kcol)),
            pl.BlockSpec((1, w, SWA_KV_W), lambda i, j, sk: (i, jnp.maximum(j - 1, 0), vcol)),
            pl.BlockSpec((1, w, SWA_KV_W), lambda i, j, sk: (i, j, vcol)),
        ],
        out_specs=pl.BlockSpec((1, w, SWA_Q_W), lambda i, j, sk: (i, j, 0)),
    )
    return pl.pallas_call(
        _swa_kernel,
        grid_spec=grid_spec,
        out_shape=jax.ShapeDtypeStruct((b, s, SWA_Q_W), _BF16),
        compiler_params=_params("arbitrary", "arbitrary"),
        name="swa",
    )(sink_flat, qkv, qkv, qkv, qkv, qkv)


def _fox_kernel(q_ref, kaug_ref, v_ref, ccol_ref, o_ref, m_ref, acc_ref, s_ref, p_ref):
    pair = pl.program_id(1)
    t = SEQ_TILE
    nq = q_ref.shape[1] // t
    lane = lax.broadcasted_iota(jnp.int32, (t, LANES), 1)
    low = lane < HEAD_DIM
    rc = FOX_ROW_CHUNK
    low_c = lax.broadcasted_iota(jnp.int32, (rc, LANES), 1) < HEAD_DIM
    ones0 = jnp.where(low, 1.0, 0.0).astype(_BF16)
    ones1 = jnp.where(low, 0.0, 1.0).astype(_BF16)

    def q_aug(qi):
        q = q_ref[0, qi * t:(qi + 1) * t, :].astype(_F32)
        ccol = ccol_ref[0, qi * t:(qi + 1) * t, :]
        qa = []
        for hh in range(2):
            cq = jnp.sum(jnp.where(lane == 2 * pair + hh, ccol, 0.0), axis=1, keepdims=True)
            base = HEAD_DIM if hh == 0 else 0
            bias = jnp.where(lane == base + CQ_POS, cq, 0.0)
            bias = jnp.where((lane >= base + CK_POS) & (lane < base + CK_POS + CK_PIECES), 1.0, bias)
            qa.append(jnp.where(low if hh == 0 else ~low, q, bias).astype(_BF16))
        return qa

    def logits(slot, qa, j):
        for hh in range(2):
            s_ref[slot, hh] = _dot(qa[hh], kaug_ref[0, hh, j])

    def absorb(slot, j, masked):
        v = v_ref[0, j * t:(j + 1) * t, :]
        zero = jnp.zeros_like(v)
        vcat = jnp.concatenate([jnp.concatenate([jnp.where(low, v, zero), ones0], axis=1),
                                jnp.concatenate([jnp.where(low, zero, v), ones1], axis=1)], axis=0)
        for r in range(t // rc):
            rows = pl.ds(r * rc, rc)
            if masked:
                keep = (lax.broadcasted_iota(jnp.int32, (rc, t), 1)
                        <= lax.broadcasted_iota(jnp.int32, (rc, t), 0) + r * rc)
            alphas = []
            for hh in range(2):
                s = s_ref[slot, hh, rows, :]
                if masked:
                    s = jnp.where(keep, s, NEG_INF)
                m_old = m_ref[hh, rows, :]
                m_new = jnp.maximum(m_old, jnp.max(s, axis=1, keepdims=True))
                alpha = jnp.exp2(m_old - m_new)
                p = jnp.exp2(s - jnp.tile(m_new, (1, t // LANES)))
                m_ref[hh, rows, :] = m_new
                p_ref[rows, hh * t:(hh + 1) * t] = p.astype(_BF16)
                alphas.append(alpha)
            scale = jnp.where(low_c, alphas[0], alphas[1])
            acc_ref[rows, :] = jnp.concatenate([scale, scale], axis=1) * acc_ref[rows, :]
        acc_ref[...] = acc_ref[...] + _dot(p_ref[...], vcat)

    visits = [(qi, j) for qi in range(nq) for j in range(qi + 1)]
    qa = q_aug(0)
    logits(0, qa, 0)
    for u, (qi, j) in enumerate(visits):
        slot = u % 2
        if j == 0:
            m_ref[...] = jnp.full_like(m_ref, NEG_INF)
            acc_ref[...] = jnp.zeros_like(acc_ref)
        if u + 1 < len(visits):
            nqi, nj = visits[u + 1]
            if nj == 0:
                qa = q_aug(nqi)
            logits(1 - slot, qa, nj)
        absorb(slot, j, j == qi)
        if j == qi:
            acc = acc_ref[...]
            o_ref[0, qi * t:(qi + 1) * t, :] = (acc[:, :LANES] / acc[:, LANES:]).astype(_BF16)


def _fox(qkv, ccol, kaug):
    b, s, _ = qkv.shape
    t = SEQ_TILE
    pairs = FOX_HEADS // 2
    qcol = (SWA_Q_W + 2 * SWA_KV_W) // LANES
    vcol = qcol + 2 * FOX_W // LANES
    return pl.pallas_call(
        _fox_kernel,
        grid=(b, pairs),
        in_specs=[
            pl.BlockSpec((1, s, LANES), lambda i, p: (i, 0, qcol + p)),
            pl.BlockSpec((1, 2, s // t, LANES, t), lambda i, p: (i, p, 0, 0, 0)),
            pl.BlockSpec((1, s, LANES), lambda i, p: (i, 0, vcol + p)),
            pl.BlockSpec((1, s, LANES), lambda i, p: (i, 0, 0)),
        ],
        out_specs=pl.BlockSpec((1, s, LANES), lambda i, p: (i, 0, p)),
        out_shape=jax.ShapeDtypeStruct((b, s, FOX_W), _BF16),
        scratch_shapes=[
            pltpu.VMEM((2, t, LANES), _F32),
            pltpu.VMEM((t, 2 * LANES), _F32),
            pltpu.VMEM((2, 2, t, t), _F32),
            pltpu.VMEM((t, 2 * t), _BF16),
        ],
        compiler_params=_params("arbitrary", "arbitrary"),
        name="fox",
    )(qkv, kaug, qkv, ccol)


def _mix_kernel(alpha, x_ref, aa_ref, af_ref, wg_ref, wps_ref, wpf_ref, wo_ref, g1_ref, b1_ref,
                wr_ref, br_ref, h_ref, hs_ref, idx_ref, gate_ref):
    d = x_ref.shape[1]
    t = x_ref.shape[0] // MIX_SPLIT
    for part in range(MIX_SPLIT):
        rows = slice(part * t, (part + 1) * t)
        x = x_ref[rows, :]
        gates = _dot(x.astype(_BF16), wg_ref[...])
        ya = _dot(aa_ref[rows, :], wps_ref[...])
        yf = _dot(af_ref[rows, :], wpf_ref[...])
        mix = jax.nn.sigmoid(gates[:, :d]) * ya + jax.nn.sigmoid(gates[:, d:]) * yf
        z = alpha * x + _dot(mix.astype(_BF16), wo_ref[...])
        h = _layer_norm(z, g1_ref[...], b1_ref[...])
        h_ref[rows, :] = h
        c = d // LANES
        for j in range(c):
            hs_ref[pl.ds(part * t * c + j, t, stride=c), :] = h[:, j * LANES:(j + 1) * LANES]

        logits = _dot(h.astype(_BF16), wr_ref[...]) + br_ref[...]
        lt = logits.T[:N_EXPERTS, :]
        eidx = lax.broadcasted_iota(jnp.int32, (N_EXPERTS, t), 0)
        work = lt
        vals, idxs = [], []
        for _ in range(TOP_K):
            mk = jnp.max(work, axis=0, keepdims=True)
            ik = jnp.min(jnp.where(work == mk, eidx, N_EXPERTS), axis=0, keepdims=True)
            vals.append(mk)
            idxs.append(ik)
            work = jnp.where(eidx == ik, NEG_INF, work)
        ex = [jnp.exp(vk - vals[0]) for vk in vals]
        denom = ex[0] + ex[1] + ex[2] + ex[3]
        idx_ref[:, rows] = jnp.concatenate(idxs + [jnp.zeros((8 - TOP_K, t), jnp.int32)], axis=0)
        gt = jnp.concatenate([e / denom for e in ex] + [jnp.zeros((LANES - TOP_K, t), _F32)], axis=0)
        gate_ref[rows, :] = gt.T


def _mix(alpha, x2, att_a, att_f, wg, wps, wpf, wo, g1, b1, wr, br):
    n, d = x2.shape
    t = SEQ_TILE
    const = lambda shape: pl.BlockSpec(shape, lambda i: (0, 0))
    return pl.pallas_call(
        functools.partial(_mix_kernel, alpha),
        grid=(n // t,),
        in_specs=[
            pl.BlockSpec((t, d), lambda i: (i, 0)),
            pl.BlockSpec((t, SWA_Q_W), lambda i: (i, 0)),
            pl.BlockSpec((t, FOX_W), lambda i: (i, 0)),
            const(wg.shape), const(wps.shape), const(wpf.shape), const(wo.shape),
            const(g1.shape), const(b1.shape), const(wr.shape), const(br.shape),
        ],
        out_specs=[
            pl.BlockSpec((t, d), lambda i: (i, 0)),
            pl.BlockSpec((t * d // LANES, LANES), lambda i: (i, 0)),
            pl.BlockSpec((8, t), lambda i: (0, i)),
            pl.BlockSpec((t, LANES), lambda i: (i, 0)),
        ],
        out_shape=[
            jax.ShapeDtypeStruct((n, d), _F32),
            jax.ShapeDtypeStruct((n * d // LANES, LANES), _F32),
            jax.ShapeDtypeStruct((8, n), jnp.int32),
            jax.ShapeDtypeStruct((n, LANES), _F32),
        ],
        compiler_params=_params("arbitrary"),
        name="mix",
    )(x2, att_a, att_f, wg, wps, wpf, wo, g1, b1, wr, br)


def _route_kernel(idx_ref, tri_ref, slot_ref, meta_ref, cnt_ref, run_ref):
    phase = pl.program_id(0)
    i = pl.program_id(1)
    t = idx_ref.shape[1]
    blk = EXPERT_ROWS
    eidx = lax.broadcasted_iota(jnp.int32, (N_EXPERTS, t), 0)
    idx = idx_ref[...]
    sel = (eidx == idx[0:1, :])
    for k in range(1, TOP_K):
        sel = sel | (eidx == idx[k:k + 1, :])
    self32 = jnp.where(sel, 1.0, 0.0)
    tile_cnt = jnp.sum(self32, axis=1, keepdims=True)

    @pl.when((phase == 0) & (i == 0))
    def _():
        cnt_ref[...] = jnp.zeros_like(cnt_ref)

    @pl.when(phase == 0)
    def _():
        cnt_ref[...] = cnt_ref[...] + tile_cnt

    @pl.when((phase == 1) & (i == 0))
    def _():
        cnt = cnt_ref[...]
        padded = jnp.floor((cnt + (blk - 1)) * (1.0 / blk)) * blk
        run = jnp.zeros((1, LANES), _F32)
        rows = []
        for e in range(N_EXPERTS):
            rows.append(run)
            run = run + padded[e:e + 1, :]
        start = jnp.concatenate(rows, axis=0)
        run_ref[...] = start
        nbp = meta_ref.shape[1]
        s1 = start[:, 0:1]
        e1 = s1 + padded[:, 0:1]
        c1 = cnt[:, 0:1]
        row0 = (lax.broadcasted_iota(jnp.int32, (N_EXPERTS, nbp), 1) * blk).astype(_F32)
        owner = jnp.sum(jnp.where(e1 <= row0, 1.0, 0.0), axis=0, keepdims=True)
        owner = jnp.minimum(owner, N_EXPERTS - 1.0)
        inside = (s1 <= row0) & (row0 < e1)
        nval = jnp.sum(jnp.where(inside, jnp.clip(c1 - (row0 - s1), 0.0, blk), 0.0), axis=0, keepdims=True)
        erow = lax.broadcasted_iota(jnp.int32, (N_EXPERTS, nbp), 0).astype(_F32)
        later = (erow > owner) & (c1 > 0.0)
        nxt = jnp.min(jnp.where(later, erow, float(N_EXPERTS)), axis=0, keepdims=True)
        meta_ref[...] = jnp.concatenate(
            [owner.astype(jnp.int32), nval.astype(jnp.int32), nxt.astype(jnp.int32),
             jnp.zeros((5, nbp), jnp.int32)], axis=0)

    @pl.when(phase == 1)
    def _():
        rank = _dot(self32.astype(_BF16), tri_ref[...])
        slot = (run_ref[:, 0:1] + rank).astype(jnp.int32)
        for k in range(TOP_K):
            sk = jnp.sum(jnp.where(eidx == idx[k:k + 1, :], slot, 0), axis=0, keepdims=True)
            slot_ref[0, :, k * t:(k + 1) * t] = sk
        run_ref[...] = run_ref[...] + tile_cnt


def _route(idx_t, tri_strict, n_blocks_pad):
    n = idx_t.shape[1]
    t = SEQ_TILE
    nt = n // t
    return pl.pallas_call(
        _route_kernel,
        grid=(2, nt),
        in_specs=[
            pl.BlockSpec((8, t), lambda ph, i: (0, i)),
            pl.BlockSpec((t, t), lambda ph, i: (0, 0)),
        ],
        out_specs=[
            pl.BlockSpec((1, 1, TOP_K * t), lambda ph, i: (ph * i, 0, 0)),
            pl.BlockSpec((8, n_blocks_pad), lambda ph, i: (0, 0)),
        ],
        out_shape=[
            jax.ShapeDtypeStruct((nt, 1, TOP_K * t), jnp.int32),
            jax.ShapeDtypeStruct((8, n_blocks_pad), jnp.int32),
        ],
        scratch_shapes=[
            pltpu.VMEM((N_EXPERTS, LANES), _F32),
            pltpu.VMEM((N_EXPERTS, LANES), _F32),
        ],
        compiler_params=_params("arbitrary", "arbitrary"),
        name="route",
    )(idx_t, tri_strict)


SLAB = 8
DRAIN_UNROLL = 8
COMBINE_PARTS = 2


def _row_copy(src_ref, src_row, dst_ref, dst_row, sem):
    src = src_ref.at[pl.ds(pl.multiple_of(src_row * SLAB, SLAB), SLAB), :]
    dst = dst_ref.at[pl.ds(pl.multiple_of(dst_row * SLAB, SLAB), SLAB), :]
    return pltpu.make_async_copy(src, dst, sem)


def _dispatch_kernel(meta_ref, slot_ref, h_ref, xs_ref, zero_buf, sem, zsem):
    t = h_ref.shape[0] // SLAB
    blk = zero_buf.shape[0]

    @pl.when(pl.program_id(0) == 0)
    def _():
        zero_buf[...] = jnp.zeros_like(zero_buf)
        n_blocks = xs_ref.shape[0] // blk

        def fill(bi):
            return pltpu.make_async_copy(zero_buf, xs_ref.at[pl.ds(pl.multiple_of(bi * blk, blk), blk), :], zsem)

        def start(bi, carry):
            @pl.when(meta_ref[1, bi] < EXPERT_ROWS)
            def _():
                fill(bi).start()
            return carry

        def wait(bi, carry):
            @pl.when(meta_ref[1, bi] < EXPERT_ROWS)
            def _():
                fill(bi).wait()
            return carry

        lax.fori_loop(0, n_blocks, start, 0)
        lax.fori_loop(0, n_blocks, wait, 0)

    def issue(r, carry):
        for k in range(TOP_K):
            _row_copy(h_ref, r, xs_ref, slot_ref[0, 0, k * t + r], sem).start(priority=k % 2)
        return carry

    lax.fori_loop(0, t, issue, 0)

    def drain(r8, carry):
        for u in range(DRAIN_UNROLL):
            for k in range(TOP_K):
                _row_copy(h_ref, r8 * DRAIN_UNROLL + u, xs_ref, 0, sem).wait()
        return carry

    lax.fori_loop(0, t // DRAIN_UNROLL, drain, 0)


def _dispatch(meta, slots, hs, n_rows):
    n = hs.shape[0] // SLAB
    t = SEQ_TILE
    grid_spec = pltpu.PrefetchScalarGridSpec(
        num_scalar_prefetch=1,
        grid=(n // t,),
        in_specs=[
            pl.BlockSpec((1, 1, TOP_K * t), lambda i, m: (i, 0, 0), memory_space=pltpu.SMEM),
            pl.BlockSpec((t * SLAB, LANES), lambda i, m: (i, 0)),
        ],
        out_specs=pl.BlockSpec(memory_space=pl.ANY),
        scratch_shapes=[pltpu.VMEM((EXPERT_ROWS * SLAB, LANES), _F32), pltpu.SemaphoreType.DMA(()),
                        pltpu.SemaphoreType.DMA(())],
    )
    return pl.pallas_call(
        _dispatch_kernel,
        grid_spec=grid_spec,
        out_shape=jax.ShapeDtypeStruct((n_rows * SLAB, LANES), _F32),
        compiler_params=_params("arbitrary"),
        name="dispatch",
    )(meta, slots, hs)


def _expert_kernel(meta_ref, x_ref, wgu_hbm, bgu_ref, wd_hbm, bd_ref, y_ref,
                   wgu_land, wd_land, wgu_bf, wd_bf, sems):
    b = pl.program_id(0)
    e = meta_ref[0, b]
    nval = meta_ref[1, b]
    prev = meta_ref[0, jnp.maximum(b - 1, 0)]
    f = wd_bf.shape[0]

    def fetch(expert):
        return (pltpu.make_async_copy(wgu_hbm.at[expert], wgu_land, sems.at[0]),
                pltpu.make_async_copy(wd_hbm.at[expert], wd_land, sems.at[1]))

    @pl.when(b == 0)
    def _():
        for cp in fetch(e):
            cp.start()

    @pl.when((nval > 0) & ((b == 0) | (prev != e)))
    def _():
        for cp in fetch(e):
            cp.wait()

        def convert(i, carry):
            rows = pl.ds(pl.multiple_of(i * CAST_ROWS, CAST_ROWS), CAST_ROWS)
            wgu_bf[rows, :] = wgu_land[rows, :].astype(_BF16)
            wd_bf[rows, :] = wd_land[rows, :].astype(_BF16)
            return carry

        lax.fori_loop(0, wgu_land.shape[0] // CAST_ROWS, convert, 0)
        nxt = meta_ref[2, b]

        @pl.when(nxt < N_EXPERTS)
        def _():
            for cp in fetch(nxt):
                cp.start()

    def swiglu(rows):
        x = _load_slabs(x_ref, rows, SLAB).astype(_BF16)
        hgu = _dot(x, wgu_bf[...]) + bgu_ref[0]
        gate = jnp.minimum(hgu[:, :f], SWIGLU_LIMIT)
        up = jnp.clip(hgu[:, f:], -SWIGLU_LIMIT, SWIGLU_LIMIT)
        act = gate * jax.nn.sigmoid(gate * SWIGLU_ALPHA) * (up + 1.0)
        _store_slabs(y_ref, _dot(act.astype(_BF16), wd_bf[...]) + bd_ref[0])

    half = EXPERT_ROWS // 2

    @pl.when(nval > half)
    def _():
        swiglu(EXPERT_ROWS)

    @pl.when((nval > 0) & (nval <= half))
    def _():
        swiglu(half)
        y_ref[half * SLAB:, :] = jnp.zeros((half * SLAB, LANES), _F32)

    @pl.when(nval <= 0)
    def _():
        y_ref[...] = jnp.zeros_like(y_ref)


def _experts(meta, xs, w_gu, b_gu, w_d, b_d):
    ne, d, f2 = w_gu.shape
    assert d == SLAB * LANES
    r = xs.shape[0] // SLAB
    f = w_d.shape[1]
    blk = EXPERT_ROWS
    grid_spec = pltpu.PrefetchScalarGridSpec(
        num_scalar_prefetch=1,
        grid=(r // blk,),
        in_specs=[
            pl.BlockSpec((blk * SLAB, LANES), lambda i, m: (i, 0)),
            pl.BlockSpec(memory_space=pl.ANY),
            pl.BlockSpec((1, 1, f2), lambda i, m: (m[0, i], 0, 0)),
            pl.BlockSpec(memory_space=pl.ANY),
            pl.BlockSpec((1, 1, d), lambda i, m: (m[0, i], 0, 0)),
        ],
        out_specs=pl.BlockSpec((blk * SLAB, LANES), lambda i, m: (i, 0)),
        scratch_shapes=[
            pltpu.VMEM((d, f2), _F32), pltpu.VMEM((f, d), _F32),
            pltpu.VMEM((d, f2), _BF16), pltpu.VMEM((f, d), _BF16),
            pltpu.SemaphoreType.DMA((2,)),
        ],
    )
    return pl.pallas_call(
        _expert_kernel,
        grid_spec=grid_spec,
        out_shape=jax.ShapeDtypeStruct((r * SLAB, LANES), _F32),
        compiler_params=_params("arbitrary"),
        name="experts",
    )(meta, xs, w_gu, b_gu.reshape(ne, 1, f2), w_d, b_d.reshape(ne, 1, d))


def _combine_kernel(alpha, slot_ref, h_ref, gate_ref, g2_ref, b2_ref, ys_ref, o_ref, buf, sems):
    t = h_ref.shape[0]
    tp = t // COMBINE_PARTS

    def issue(part):
        def body(r, carry):
            tok = part * tp + r
            for k in range(TOP_K):
                _row_copy(ys_ref, slot_ref[0, 0, k * t + tok], buf.at[k], tok, sems.at[part]).start(priority=k % 2)
            return carry

        lax.fori_loop(0, tp, body, 0)

    def drain(part):
        def body(r8, carry):
            for u in range(DRAIN_UNROLL):
                for k in range(TOP_K):
                    _row_copy(ys_ref, 0, buf.at[k], part * tp + r8 * DRAIN_UNROLL + u, sems.at[part]).wait()
            return carry

        lax.fori_loop(0, tp // DRAIN_UNROLL, body, 0)

    def finish(part):
        rows = slice(part * tp, (part + 1) * tp)
        gate = gate_ref[rows, :]
        moe = gate[:, 0:1] * _load_slabs(buf, tp, SLAB, (0,), part * tp)
        for k in range(1, TOP_K):
            moe = moe + gate[:, k:k + 1] * _load_slabs(buf, tp, SLAB, (k,), part * tp)
        o_ref[rows, :] = _layer_norm(alpha * h_ref[rows, :] + moe, g2_ref[...], b2_ref[...])

    for part in range(COMBINE_PARTS):
        issue(part)
    for part in range(COMBINE_PARTS):
        drain(part)
        finish(part)


def _combine(alpha, slots, h, gate, g2, b2, ys):
    n, d = h.shape
    t = SEQ_TILE
    return pl.pallas_call(
        functools.partial(_combine_kernel, alpha),
        grid=(n // t,),
        in_specs=[
            pl.BlockSpec((1, 1, TOP_K * t), lambda i: (i, 0, 0), memory_space=pltpu.SMEM),
            pl.BlockSpec((t, d), lambda i: (i, 0)),
            pl.BlockSpec((t, LANES), lambda i: (i, 0)),
            pl.BlockSpec((1, d), lambda i: (0, 0)),
            pl.BlockSpec((1, d), lambda i: (0, 0)),
            pl.BlockSpec(memory_space=pl.ANY),
        ],
        out_specs=pl.BlockSpec((t, d), lambda i: (i, 0)),
        out_shape=jax.ShapeDtypeStruct((n, d), _F32),
        scratch_shapes=[pltpu.VMEM((TOP_K, t * SLAB, LANES), _F32), pltpu.SemaphoreType.DMA((COMBINE_PARTS,))],
        compiler_params=_params("arbitrary"),
        name="combine",
    )(slots, h, gate, g2, b2, ys)


def _pad_cols(a, width):
    return jnp.pad(a, ((0, 0), (0, width - a.shape[1])))


def kernel(x, w_in, b_forget, sink, w_proj_swa, w_proj_fox, w_out, ln1_g, ln1_b, w_router, b_router,
           w_gate_up, b_gate_up, w_down, b_down, ln2_g, ln2_b):
    b, s, d = x.shape
    depth = w_in.shape[0]
    n = b * s
    alpha = float((2.0 * depth) ** 0.25)
    assert s % SEQ_TILE == 0 and SEQ_TILE % WINDOW == 0 and d % LANES == 0
    assert w_router.shape[2] == N_EXPERTS and w_in.shape[2] == ATT_W + FOX_HEADS + 2 * d

    blk = EXPERT_ROWS
    n_rows = ((n * TOP_K + blk - 1) // blk) * blk + N_EXPERTS * blk
    n_blocks_pad = ((n_rows // blk + LANES - 1) // LANES) * LANES
    t = SEQ_TILE
    ri = lax.broadcasted_iota(jnp.int32, (t, t), 0)
    ci = lax.broadcasted_iota(jnp.int32, (t, t), 1)
    tri_incl = (ci <= ri).astype(_BF16)
    tri_strict = (ri < ci).astype(_BF16)

    h = x
    for layer in range(depth):
        w_l = w_in[layer]
        w_att = _pad_cols(w_l[:, :ATT_W + FOX_HEADS], ATT_W + LANES).astype(_BF16)
        w_gates = w_l[:, ATT_W + FOX_HEADS:].astype(_BF16)
        bf_pad = _pad_cols(b_forget[layer].reshape(1, FOX_HEADS).astype(_F32), LANES)

        qkv, ccol, kaug = _in_proj(h, w_att, bf_pad, tri_incl)
        att_a = _swa(qkv, sink[layer].reshape(-1).astype(_F32))
        att_f = _fox(qkv, ccol, kaug)

        h1, h1_slabs, idx_t, gate = _mix(
            alpha, h.reshape(n, d), att_a.reshape(n, SWA_Q_W), att_f.reshape(n, FOX_W),
            w_gates, w_proj_swa[layer].astype(_BF16), w_proj_fox[layer].astype(_BF16),
            w_out[layer].astype(_BF16), ln1_g[layer].reshape(1, d), ln1_b[layer].reshape(1, d),
            _pad_cols(w_router[layer], LANES).astype(_BF16),
            _pad_cols(b_router[layer].reshape(1, N_EXPERTS).astype(_F32), LANES))

        slots, meta = _route(idx_t, tri_strict, n_blocks_pad)
        xs = _dispatch(meta, slots, h1_slabs, n_rows)
        ys = _experts(meta, xs, w_gate_up[layer], b_gate_up[layer], w_down[layer], b_down[layer])
        out = _combine(alpha, slots, h1, gate, ln2_g[layer].reshape(1, d), ln2_b[layer].reshape(1, d), ys)
        h = out.reshape(b, s, d)
    return h
```

```python
import functools

import numpy as np
import jax
import jax.numpy as jnp
from jax import lax
from jax.experimental import pallas as pl
from jax.experimental.pallas import tpu as pltpu

HEAD_DIM = 64
SWA_Q_HEADS = 8
SWA_KV_HEADS = 2
SWA_GROUP = SWA_Q_HEADS // SWA_KV_HEADS
WINDOW = 128
FOX_HEADS = 8
N_EXPERTS = 32
TOP_K = 4
SWIGLU_LIMIT = 7.0
SWIGLU_ALPHA = 1.702
LN_EPS = 1e-5

LANES = 128
SEQ_TILE = 512
EXPERT_ROWS = 512
CAST_ROWS = 64
MIX_SPLIT = 2
FOX_ROW_CHUNK = 64
VMEM_LIMIT = 56 * 1024 * 1024

SWA_Q_W = SWA_Q_HEADS * HEAD_DIM
SWA_KV_W = SWA_KV_HEADS * HEAD_DIM
FOX_W = FOX_HEADS * HEAD_DIM
ATT_W = SWA_Q_W + 2 * SWA_KV_W + 3 * FOX_W
NEG_INF = float("-inf")
LOG2E = 1.4426950408889634
FOX_Q_SCALE = HEAD_DIM ** -0.5 * LOG2E
CQ_POS = 0
CK_POS = 8
CK_PIECES = 3

_F32 = jnp.float32
_BF16 = jnp.bfloat16


def _dot(a, b):
    return jnp.dot(a, b, preferred_element_type=_F32)


def _dot_nt(a, b):
    return lax.dot_general(a, b, (((1,), (1,)), ((), ())), preferred_element_type=_F32)


def _params(*sem):
    return pltpu.CompilerParams(dimension_semantics=sem, vmem_limit_bytes=VMEM_LIMIT)


def _layer_norm(z, g, b):
    mu = jnp.mean(z, axis=-1, keepdims=True)
    zc = z - mu
    var = jnp.mean(zc * zc, axis=-1, keepdims=True)
    return zc * lax.rsqrt(var + LN_EPS) * g + b


def _store_slabs(ref, value, lead=()):
    rows, width = value.shape
    c = width // LANES
    for j in range(c):
        ref[lead + (pl.ds(j, rows, stride=c), slice(None))] = value[:, j * LANES:(j + 1) * LANES]


def _load_slabs(ref, rows, c, lead=(), first=0):
    return jnp.concatenate(
        [ref[lead + (pl.ds(first * c + j, rows, stride=c), slice(None))] for j in range(c)], axis=1)


def _split3(v):
    p1 = v.astype(_BF16).astype(_F32)
    r1 = v - p1
    p2 = r1.astype(_BF16).astype(_F32)
    p3 = (r1 - p2).astype(_BF16).astype(_F32)
    return p1, p2, p3


def _in_proj_kernel(x_ref, w_ref, bf_ref, tri_ref, qkv_ref, ccol_ref, kaug_ref, carry_ref):
    @pl.when(pl.program_id(1) == 0)
    def _():
        carry_ref[...] = jnp.zeros_like(carry_ref)

    xb = x_ref[0].astype(_BF16)
    acc = _dot(xb, w_ref[...])
    qf_col = SWA_Q_W + 2 * SWA_KV_W
    qkv_ref[0, :, :qf_col] = acc[:, :qf_col].astype(_BF16)
    qkv_ref[0, :, qf_col:qf_col + FOX_W] = (acc[:, qf_col:qf_col + FOX_W] * FOX_Q_SCALE).astype(_BF16)
    qkv_ref[0, :, qf_col + FOX_W:] = acc[:, qf_col + FOX_W:ATT_W].astype(_BF16)
    kf_col = SWA_Q_W + 2 * SWA_KV_W + FOX_W
    kt = acc[:, kf_col:kf_col + FOX_W].T
    z = acc[:, ATT_W:] + bf_ref[...]
    log_f = jnp.minimum(z, 0.0) - jnp.log1p(jnp.exp(-jnp.abs(z)))
    tri = tri_ref[...]
    p1 = log_f.astype(_BF16)
    r1 = log_f - p1.astype(_F32)
    p2 = r1.astype(_BF16)
    p3 = (r1 - p2.astype(_F32)).astype(_BF16)
    c = _dot(tri, p1) + _dot(tri, p2) + _dot(tri, p3) + carry_ref[0:1, :]
    t = c.shape[0]
    carry_ref[...] = jnp.broadcast_to(c[t - 1:t, :], carry_ref.shape)
    cl = c * LOG2E
    ccol_ref[0] = cl
    ct = cl.T
    row = lax.broadcasted_iota(jnp.int32, (CK_POS - CQ_POS, t), 0)
    ones_rows = jnp.where(row == 0, 1.0, 0.0)
    pad_rows = jnp.zeros((HEAD_DIM - 2 * (CK_POS - CQ_POS), t), _F32)
    for h in range(FOX_HEADS):
        c1, c2, c3 = _split3(ct[h:h + 1, :])
        ck_rows = jnp.where(row == 0, -c1, jnp.where(row == 1, -c2, jnp.where(row == 2, -c3, 0.0)))
        bias_half = jnp.concatenate([ones_rows, ck_rows, pad_rows], axis=0)
        kh = kt[h * HEAD_DIM:(h + 1) * HEAD_DIM, :]
        halves = [kh, bias_half] if h % 2 == 0 else [bias_half, kh]
        kaug_ref[0, h, 0] = jnp.concatenate(halves, axis=0).astype(_BF16)


def _in_proj(x, w_att, bf_pad, tri):
    b, s, d = x.shape
    t = SEQ_TILE
    wn = w_att.shape[1]
    return pl.pallas_call(
        _in_proj_kernel,
        grid=(b, s // t),
        in_specs=[
            pl.BlockSpec((1, t, d), lambda i, j: (i, j, 0)),
            pl.BlockSpec((d, wn), lambda i, j: (0, 0)),
            pl.BlockSpec((1, LANES), lambda i, j: (0, 0)),
            pl.BlockSpec((t, t), lambda i, j: (0, 0)),
        ],
        out_specs=[
            pl.BlockSpec((1, t, ATT_W), lambda i, j: (i, j, 0)),
            pl.BlockSpec((1, t, LANES), lambda i, j: (i, j, 0)),
            pl.BlockSpec((1, FOX_HEADS, 1, LANES, t), lambda i, j: (i, 0, j, 0, 0)),
        ],
        out_shape=[
            jax.ShapeDtypeStruct((b, s, ATT_W), _BF16),
            jax.ShapeDtypeStruct((b, s, LANES), _F32),
            jax.ShapeDtypeStruct((b, FOX_HEADS, s // t, LANES, t), _BF16),
        ],
        scratch_shapes=[pltpu.VMEM((8, LANES), _F32)],
        compiler_params=_params("arbitrary", "arbitrary"),
        name="in_proj",
    )(x, w_att, bf_pad, tri)


def _alibi_slopes(n):
    return [float(v) for v in np.asarray(2.0 ** (-8.0 * np.arange(1, n + 1) / n), dtype=np.float32)]


def _swa_kernel(sink_ref, q_ref, kp_ref, kc_ref, vp_ref, vc_ref, o_ref):
    blk = pl.program_id(1)
    w = WINDOW
    q = q_ref[0] * jnp.asarray(HEAD_DIM ** -0.5, _BF16)
    k = jnp.concatenate([kp_ref[0], kc_ref[0]], axis=0)
    v = jnp.concatenate([vp_ref[0], vc_ref[0]], axis=0)
    row = lax.broadcasted_iota(jnp.int32, (w, 2 * w), 0)
    col = lax.broadcasted_iota(jnp.int32, (w, 2 * w), 1)
    dist = row + w - col
    valid = (dist >= 0) & (dist < w) & ((col >= w) | (blk > 0))
    distf = dist.astype(_F32)
    slopes = _alibi_slopes(SWA_Q_HEADS)
    outs = []
    for h in range(SWA_Q_HEADS):
        g = h // SWA_GROUP
        qh = q[:, h * HEAD_DIM:(h + 1) * HEAD_DIM]
        kg = k[:, g * HEAD_DIM:(g + 1) * HEAD_DIM]
        vg = v[:, g * HEAD_DIM:(g + 1) * HEAD_DIM]
        sc = _dot_nt(qh, kg) - slopes[h] * distf
        sc = jnp.where(valid, sc, NEG_INF)
        sk = sink_ref[h]
        m = jnp.maximum(jnp.max(sc, axis=1, keepdims=True), sk)
        p = jnp.exp(sc - m)
        denom = jnp.sum(p, axis=1, keepdims=True) + jnp.exp(sk - m)
        o = _dot(p.astype(_BF16), vg)
        outs.append(o / denom)
    o_ref[0] = jnp.concatenate(outs, axis=1).astype(_BF16)


def _swa(qkv, sink_flat):
    b, s, _ = qkv.shape
    w = WINDOW
    kcol = SWA_Q_W // LANES
    vcol = kcol + SWA_KV_W // LANES
    grid_spec = pltpu.PrefetchScalarGridSpec(
        num_scalar_prefetch=1,
        grid=(b, s // w),
        in_specs=[
            pl.BlockSpec((1, w, SWA_Q_W), lambda i, j, sk: (i, j, 0)),
            pl.BlockSpec((1, w, SWA_KV_W), lambda i, j, sk: (i, jnp.maximum(j - 1, 0), kcol)),
            pl.BlockSpec((1, w, SWA_KV_W), lambda i, j, sk: (i, j, kcol)),
            pl.BlockSpec((1, w, SWA_KV_W), lambda i, j, sk: (i, jnp.maximum(j - 1, 0), vcol)),
            pl.BlockSpec((1, w, SWA_KV_W), lambda i, j, sk: (i, j, vcol)),
        ],
        out_specs=pl.BlockSpec((1, w, SWA_Q_W), lambda i, j, sk: (i, j, 0)),
    )
    return pl.pallas_call(
        _swa_kernel,
        grid_spec=grid_spec,
        out_shape=jax.ShapeDtypeStruct((b, s, SWA_Q_W), _BF16),
        compiler_params=_params("arbitrary", "arbitrary"),
        name="swa",
    )(sink_flat, qkv, qkv, qkv, qkv, qkv)


def _fox_kernel(q_ref, kaug_ref, v_ref, ccol_ref, o_ref, m_ref, acc_ref, s_ref, p_ref):
    pair = pl.program_id(1)
    t = SEQ_TILE
    nq = q_ref.shape[1] // t
    lane = lax.broadcasted_iota(jnp.int32, (t, LANES), 1)
    low = lane < HEAD_DIM
    rc = FOX_ROW_CHUNK
    low_c = lax.broadcasted_iota(jnp.int32, (rc, LANES), 1) < HEAD_DIM
    ones0 = jnp.where(low, 1.0, 0.0).astype(_BF16)
    ones1 = jnp.where(low, 0.0, 1.0).astype(_BF16)

    def q_aug(qi):
        q = q_ref[0, qi * t:(qi + 1) * t, :].astype(_F32)
        ccol = ccol_ref[0, qi * t:(qi + 1) * t, :]
        qa = []
        for hh in range(2):
            cq = jnp.sum(jnp.where(lane == 2 * pair + hh, ccol, 0.0), axis=1, keepdims=True)
            base = HEAD_DIM if hh == 0 else 0
            bias = jnp.where(lane == base + CQ_POS, cq, 0.0)
            bias = jnp.where((lane >= base + CK_POS) & (lane < base + CK_POS + CK_PIECES), 1.0, bias)
            qa.append(jnp.where(low if hh == 0 else ~low, q, bias).astype(_BF16))
        return qa

    def logits(slot, qa, j):
        for hh in range(2):
            s_ref[slot, hh] = _dot(qa[hh], kaug_ref[0, hh, j])

    def absorb(slot, j, masked):
        v = v_ref[0, j * t:(j + 1) * t, :]
        zero = jnp.zeros_like(v)
        vcat = jnp.concatenate([jnp.concatenate([jnp.where(low, v, zero), ones0], axis=1),
                                jnp.concatenate([jnp.where(low, zero, v), ones1], axis=1)], axis=0)
        for r in range(t // rc):
            rows = pl.ds(r * rc, rc)
            if masked:
                keep = (lax.broadcasted_iota(jnp.int32, (rc, t), 1)
                        <= lax.broadcasted_iota(jnp.int32, (rc, t), 0) + r * rc)
            alphas = []
            for hh in range(2):
                s = s_ref[slot, hh, rows, :]
                if masked:
                    s = jnp.where(keep, s, NEG_INF)
                m_old = m_ref[hh, rows, :]
                m_new = jnp.maximum(m_old, jnp.max(s, axis=1, keepdims=True))
                alpha = jnp.exp2(m_old - m_new)
                p = jnp.exp2(s - jnp.tile(m_new, (1, t // LANES)))
                m_ref[hh, rows, :] = m_new
                p_ref[rows, hh * t:(hh + 1) * t] = p.astype(_BF16)
                alphas.append(alpha)
            scale = jnp.where(low_c, alphas[0], alphas[1])
            acc_ref[rows, :] = jnp.concatenate([scale, scale], axis=1) * acc_ref[rows, :]
        acc_ref[...] = acc_ref[...] + _dot(p_ref[...], vcat)

    visits = [(qi, j) for qi in range(nq) for j in range(qi + 1)]
    qa = q_aug(0)
    logits(0, qa, 0)
    for u, (qi, j) in enumerate(visits):
        slot = u % 2
        if j == 0:
            m_ref[...] = jnp.full_like(m_ref, NEG_INF)
            acc_ref[...] = jnp.zeros_like(acc_ref)
        if u + 1 < len(visits):
            nqi, nj = visits[u + 1]
            if nj == 0:
                qa = q_aug(nqi)
            logits(1 - slot, qa, nj)
        absorb(slot, j, j == qi)
        if j == qi:
            acc = acc_ref[...]
            o_ref[0, qi * t:(qi + 1) * t, :] = (acc[:, :LANES] / acc[:, LANES:]).astype(_BF16)


def _fox(qkv, ccol, kaug):
    b, s, _ = qkv.shape
    t = SEQ_TILE
    pairs = FOX_HEADS // 2
    qcol = (SWA_Q_W + 2 * SWA_KV_W) // LANES
    vcol = qcol + 2 * FOX_W // LANES
    return pl.pallas_call(
        _fox_kernel,
        grid=(b, pairs),
        in_specs=[
            pl.BlockSpec((1, s, LANES), lambda i, p: (i, 0, qcol + p)),
            pl.BlockSpec((1, 2, s // t, LANES, t), lambda i, p: (i, p, 0, 0, 0)),
            pl.BlockSpec((1, s, LANES), lambda i, p: (i, 0, vcol + p)),
            pl.BlockSpec((1, s, LANES), lambda i, p: (i, 0, 0)),
        ],
        out_specs=pl.BlockSpec((1, s, LANES), lambda i, p: (i, 0, p)),
        out_shape=jax.ShapeDtypeStruct((b, s, FOX_W), _BF16),
        scratch_shapes=[
            pltpu.VMEM((2, t, LANES), _F32),
            pltpu.VMEM((t, 2 * LANES), _F32),
            pltpu.VMEM((2, 2, t, t), _F32),
            pltpu.VMEM((t, 2 * t), _BF16),
        ],
        compiler_params=_params("arbitrary", "arbitrary"),
        name="fox",
    )(qkv, kaug, qkv, ccol)


def _mix_kernel(alpha, x_ref, aa_ref, af_ref, wg_ref, wps_ref, wpf_ref, wo_ref, g1_ref, b1_ref,
                wr_ref, br_ref, h_ref, hs_ref, idx_ref, gate_ref):
    d = x_ref.shape[1]
    t = x_ref.shape[0] // MIX_SPLIT
    for part in range(MIX_SPLIT):
        rows = slice(part * t, (part + 1) * t)
        x = x_ref[rows, :]
        gates = _dot(x.astype(_BF16), wg_ref[...])
        ya = _dot(aa_ref[rows, :], wps_ref[...])
        yf = _dot(af_ref[rows, :], wpf_ref[...])
        mix = jax.nn.sigmoid(gates[:, :d]) * ya + jax.nn.sigmoid(gates[:, d:]) * yf
        z = alpha * x + _dot(mix.astype(_BF16), wo_ref[...])
        h = _layer_norm(z, g1_ref[...], b1_ref[...])
        h_ref[rows, :] = h
        c = d // LANES
        for j in range(c):
            hs_ref[pl.ds(part * t * c + j, t, stride=c), :] = h[:, j * LANES:(j + 1) * LANES]

        logits = _dot(h.astype(_BF16), wr_ref[...]) + br_ref[...]
        lt = logits.T[:N_EXPERTS, :]
        eidx = lax.broadcasted_iota(jnp.int32, (N_EXPERTS, t), 0)
        work = lt
        vals, idxs = [], []
        for _ in range(TOP_K):
            mk = jnp.max(work, axis=0, keepdims=True)
            ik = jnp.min(jnp.where(work == mk, eidx, N_EXPERTS), axis=0, keepdims=True)
            vals.append(mk)
            idxs.append(ik)
            work = jnp.where(eidx == ik, NEG_INF, work)
        ex = [jnp.exp(vk - vals[0]) for vk in vals]
        denom = ex[0] + ex[1] + ex[2] + ex[3]
        idx_ref[:, rows] = jnp.concatenate(idxs + [jnp.zeros((8 - TOP_K, t), jnp.int32)], axis=0)
        gt = jnp.concatenate([e / denom for e in ex] + [jnp.zeros((LANES - TOP_K, t), _F32)], axis=0)
        gate_ref[rows, :] = gt.T


def _mix(alpha, x2, att_a, att_f, wg, wps, wpf, wo, g1, b1, wr, br):
    n, d = x2.shape
    t = SEQ_TILE
    const = lambda shape: pl.BlockSpec(shape, lambda i: (0, 0))
    return pl.pallas_call(
        functools.partial(_mix_kernel, alpha),
        grid=(n // t,),
        in_specs=[
            pl.BlockSpec((t, d), lambda i: (i, 0)),
            pl.BlockSpec((t, SWA_Q_W), lambda i: (i, 0)),
            pl.BlockSpec((t, FOX_W), lambda i: (i, 0)),
            const(wg.shape), const(wps.shape), const(wpf.shape), const(wo.shape),
            const(g1.shape), const(b1.shape), const(wr.shape), const(br.shape),
        ],
        out_specs=[
            pl.BlockSpec((t, d), lambda i: (i, 0)),
            pl.BlockSpec((t * d // LANES, LANES), lambda i: (i, 0)),
            pl.BlockSpec((8, t), lambda i: (0, i)),
            pl.BlockSpec((t, LANES), lambda i: (i, 0)),
        ],
        out_shape=[
            jax.ShapeDtypeStruct((n, d), _F32),
            jax.ShapeDtypeStruct((n * d // LANES, LANES), _F32),
            jax.ShapeDtypeStruct((8, n), jnp.int32),
            jax.ShapeDtypeStruct((n, LANES), _F32),
        ],
        compiler_params=_params("arbitrary"),
        name="mix",
    )(x2, att_a, att_f, wg, wps, wpf, wo, g1, b1, wr, br)


def _route_kernel(idx_ref, tri_ref, slot_ref, meta_ref, cnt_ref, run_ref):
    phase = pl.program_id(0)
    i = pl.program_id(1)
    t = idx_ref.shape[1]
    blk = EXPERT_ROWS
    eidx = lax.broadcasted_iota(jnp.int32, (N_EXPERTS, t), 0)
    idx = idx_ref[...]
    sel = (eidx == idx[0:1, :])
    for k in range(1, TOP_K):
        sel = sel | (eidx == idx[k:k + 1, :])
    self32 = jnp.where(sel, 1.0, 0.0)
    tile_cnt = jnp.sum(self32, axis=1, keepdims=True)

    @pl.when((phase == 0) & (i == 0))
    def _():
        cnt_ref[...] = jnp.zeros_like(cnt_ref)

    @pl.when(phase == 0)
    def _():
        cnt_ref[...] = cnt_ref[...] + tile_cnt

    @pl.when((phase == 1) & (i == 0))
    def _():
        cnt = cnt_ref[...]
        padded = jnp.floor((cnt + (blk - 1)) * (1.0 / blk)) * blk
        run = jnp.zeros((1, LANES), _F32)
        rows = []
        for e in range(N_EXPERTS):
            rows.append(run)
            run = run + padded[e:e + 1, :]
        start = jnp.concatenate(rows, axis=0)
        run_ref[...] = start
        nbp = meta_ref.shape[1]
        s1 = start[:, 0:1]
        e1 = s1 + padded[:, 0:1]
        c1 = cnt[:, 0:1]
        row0 = (lax.broadcasted_iota(jnp.int32, (N_EXPERTS, nbp), 1) * blk).astype(_F32)
        owner = jnp.sum(jnp.where(e1 <= row0, 1.0, 0.0), axis=0, keepdims=True)
        owner = jnp.minimum(owner, N_EXPERTS - 1.0)
        inside = (s1 <= row0) & (row0 < e1)
        nval = jnp.sum(jnp.where(inside, jnp.clip(c1 - (row0 - s1), 0.0, blk), 0.0), axis=0, keepdims=True)
        erow = lax.broadcasted_iota(jnp.int32, (N_EXPERTS, nbp), 0).astype(_F32)
        later = (erow > owner) & (c1 > 0.0)
        nxt = jnp.min(jnp.where(later, erow, float(N_EXPERTS)), axis=0, keepdims=True)
        meta_ref[...] = jnp.concatenate(
            [owner.astype(jnp.int32), nval.astype(jnp.int32), nxt.astype(jnp.int32),
             jnp.zeros((5, nbp), jnp.int32)], axis=0)

    @pl.when(phase == 1)
    def _():
        rank = _dot(self32.astype(_BF16), tri_ref[...])
        slot = (run_ref[:, 0:1] + rank).astype(jnp.int32)
        for k in range(TOP_K):
            sk = jnp.sum(jnp.where(eidx == idx[k:k + 1, :], slot, 0), axis=0, keepdims=True)
            slot_ref[0, :, k * t:(k + 1) * t] = sk
        run_ref[...] = run_ref[...] + tile_cnt


def _route(idx_t, tri_strict, n_blocks_pad):
    n = idx_t.shape[1]
    t = SEQ_TILE
    nt = n // t
    return pl.pallas_call(
        _route_kernel,
        grid=(2, nt),
        in_specs=[
            pl.BlockSpec((8, t), lambda ph, i: (0, i)),
            pl.BlockSpec((t, t), lambda ph, i: (0, 0)),
        ],
        out_specs=[
            pl.BlockSpec((1, 1, TOP_K * t), lambda ph, i: (ph * i, 0, 0)),
            pl.BlockSpec((8, n_blocks_pad), lambda ph, i: (0, 0)),
        ],
        out_shape=[
            jax.ShapeDtypeStruct((nt, 1, TOP_K * t), jnp.int32),
            jax.ShapeDtypeStruct((8, n_blocks_pad), jnp.int32),
        ],
        scratch_shapes=[
            pltpu.VMEM((N_EXPERTS, LANES), _F32),
            pltpu.VMEM((N_EXPERTS, LANES), _F32),
        ],
        compiler_params=_params("arbitrary", "arbitrary"),
        name="route",
    )(idx_t, tri_strict)


SLAB = 8
DRAIN_UNROLL = 8
COMBINE_PARTS = 2


def _row_copy(src_ref, src_row, dst_ref, dst_row, sem):
    src = src_ref.at[pl.ds(pl.multiple_of(src_row * SLAB, SLAB), SLAB), :]
    dst = dst_ref.at[pl.ds(pl.multiple_of(dst_row * SLAB, SLAB), SLAB), :]
    return pltpu.make_async_copy(src, dst, sem)


def _dispatch_kernel(meta_ref, slot_ref, h_ref, xs_ref, zero_buf, sem, zsem):
    t = h_ref.shape[0] // SLAB
    blk = zero_buf.shape[0]

    @pl.when(pl.program_id(0) == 0)
    def _():
        zero_buf[...] = jnp.zeros_like(zero_buf)
        n_blocks = xs_ref.shape[0] // blk

        def fill(bi):
            return pltpu.make_async_copy(zero_buf, xs_ref.at[pl.ds(pl.multiple_of(bi * blk, blk), blk), :], zsem)

        def start(bi, carry):
            @pl.when(meta_ref[1, bi] < EXPERT_ROWS)
            def _():
                fill(bi).start()
            return carry

        def wait(bi, carry):
            @pl.when(meta_ref[1, bi] < EXPERT_ROWS)
            def _():
                fill(bi).wait()
            return carry

        lax.fori_loop(0, n_blocks, start, 0)
        lax.fori_loop(0, n_blocks, wait, 0)

    def issue(r4, carry):
        for u in range(4):
            r = r4 * 4 + u
            for k in range(TOP_K):
                _row_copy(h_ref, r, xs_ref, slot_ref[0, 0, k * t + r], sem).start(priority=k % 2)
        return carry

    lax.fori_loop(0, t // 4, issue, 0)

    def drain(r8, carry):
        for u in range(DRAIN_UNROLL):
            for k in range(TOP_K):
                _row_copy(h_ref, r8 * DRAIN_UNROLL + u, xs_ref, 0, sem).wait()
        return carry

    lax.fori_loop(0, t // DRAIN_UNROLL, drain, 0)


def _dispatch(meta, slots, hs, n_rows):
    n = hs.shape[0] // SLAB
    t = SEQ_TILE
    grid_spec = pltpu.PrefetchScalarGridSpec(
        num_scalar_prefetch=1,
        grid=(n // t,),
        in_specs=[
            pl.BlockSpec((1, 1, TOP_K * t), lambda i, m: (i, 0, 0), memory_space=pltpu.SMEM),
            pl.BlockSpec((t * SLAB, LANES), lambda i, m: (i, 0)),
        ],
        out_specs=pl.BlockSpec(memory_space=pl.ANY),
        scratch_shapes=[pltpu.VMEM((EXPERT_ROWS * SLAB, LANES), _F32), pltpu.SemaphoreType.DMA(()),
                        pltpu.SemaphoreType.DMA(())],
    )
    return pl.pallas_call(
        _dispatch_kernel,
        grid_spec=grid_spec,
        out_shape=jax.ShapeDtypeStruct((n_rows * SLAB, LANES), _F32),
        compiler_params=_params("arbitrary"),
        name="dispatch",
    )(meta, slots, hs)


def _expert_kernel(meta_ref, x_ref, wgu_hbm, bgu_ref, wd_hbm, bd_ref, y_ref,
                   wgu_land, wd_land, wgu_bf, wd_bf, sems):
    b = pl.program_id(0)
    e = meta_ref[0, b]
    nval = meta_ref[1, b]
    prev = meta_ref[0, jnp.maximum(b - 1, 0)]
    f = wd_bf.shape[0]

    def fetch(expert):
        return (pltpu.make_async_copy(wgu_hbm.at[expert], wgu_land, sems.at[0]),
                pltpu.make_async_copy(wd_hbm.at[expert], wd_land, sems.at[1]))

    @pl.when(b == 0)
    def _():
        for cp in fetch(e):
            cp.start()

    @pl.when((nval > 0) & ((b == 0) | (prev != e)))
    def _():
        for cp in fetch(e):
            cp.wait()

        def convert(i, carry):
            rows = pl.ds(pl.multiple_of(i * CAST_ROWS, CAST_ROWS), CAST_ROWS)
            wgu_bf[rows, :] = wgu_land[rows, :].astype(_BF16)
            wd_bf[rows, :] = wd_land[rows, :].astype(_BF16)
            return carry

        lax.fori_loop(0, wgu_land.shape[0] // CAST_ROWS, convert, 0)
        nxt = meta_ref[2, b]

        @pl.when(nxt < N_EXPERTS)
        def _():
            for cp in fetch(nxt):
                cp.start(priority=1)

    def swiglu(rows):
        x = _load_slabs(x_ref, rows, SLAB).astype(_BF16)
        hgu = _dot(x, wgu_bf[...]) + bgu_ref[0]
        gate = jnp.minimum(hgu[:, :f], SWIGLU_LIMIT)
        up = jnp.clip(hgu[:, f:], -SWIGLU_LIMIT, SWIGLU_LIMIT)
        act = gate * jax.nn.sigmoid(gate * SWIGLU_ALPHA) * (up + 1.0)
        _store_slabs(y_ref, _dot(act.astype(_BF16), wd_bf[...]) + bd_ref[0])

    half = EXPERT_ROWS // 2

    @pl.when(nval > half)
    def _():
        swiglu(EXPERT_ROWS)

    @pl.when((nval > 0) & (nval <= half))
    def _():
        swiglu(half)
        y_ref[half * SLAB:, :] = jnp.zeros((half * SLAB, LANES), _F32)

    @pl.when(nval <= 0)
    def _():
        y_ref[...] = jnp.zeros_like(y_ref)


def _experts(meta, xs, w_gu, b_gu, w_d, b_d):
    ne, d, f2 = w_gu.shape
    assert d == SLAB * LANES
    r = xs.shape[0] // SLAB
    f = w_d.shape[1]
    blk = EXPERT_ROWS
    grid_spec = pltpu.PrefetchScalarGridSpec(
        num_scalar_prefetch=1,
        grid=(r // blk,),
        in_specs=[
            pl.BlockSpec((blk * SLAB, LANES), lambda i, m: (i, 0)),
            pl.BlockSpec(memory_space=pl.ANY),
            pl.BlockSpec((1, 1, f2), lambda i, m: (m[0, i], 0, 0)),
            pl.BlockSpec(memory_space=pl.ANY),
            pl.BlockSpec((1, 1, d), lambda i, m: (m[0, i], 0, 0)),
        ],
        out_specs=pl.BlockSpec((blk * SLAB, LANES), lambda i, m: (i, 0)),
        scratch_shapes=[
            pltpu.VMEM((d, f2), _F32), pltpu.VMEM((f, d), _F32),
            pltpu.VMEM((d, f2), _BF16), pltpu.VMEM((f, d), _BF16),
            pltpu.SemaphoreType.DMA((2,)),
        ],
    )
    return pl.pallas_call(
        _expert_kernel,
        grid_spec=grid_spec,
        out_shape=jax.ShapeDtypeStruct((r * SLAB, LANES), _F32),
        compiler_params=_params("arbitrary"),
        name="experts",
    )(meta, xs, w_gu, b_gu.reshape(ne, 1, f2), w_d, b_d.reshape(ne, 1, d))


def _combine_kernel(alpha, slot_ref, h_ref, gate_ref, g2_ref, b2_ref, ys_ref, o_ref, buf, sems):
    t = h_ref.shape[0]
    tp = t // COMBINE_PARTS

    def issue(part):
        def body(r, carry):
            tok = part * tp + r
            for k in range(TOP_K):
                _row_copy(ys_ref, slot_ref[0, 0, k * t + tok], buf.at[k], tok, sems.at[part]).start(priority=k % 2)
            return carry

        lax.fori_loop(0, tp, body, 0)

    def drain(part):
        def body(r8, carry):
            for u in range(DRAIN_UNROLL):
                for k in range(TOP_K):
                    _row_copy(ys_ref, 0, buf.at[k], part * tp + r8 * DRAIN_UNROLL + u, sems.at[part]).wait()
            return carry

        lax.fori_loop(0, tp // DRAIN_UNROLL, body, 0)

    def finish(part):
        rows = slice(part * tp, (part + 1) * tp)
        gate = gate_ref[rows, :]
        moe = gate[:, 0:1] * _load_slabs(buf, tp, SLAB, (0,), part * tp)
        for k in range(1, TOP_K):
            moe = moe + gate[:, k:k + 1] * _load_slabs(buf, tp, SLAB, (k,), part * tp)
        o_ref[rows, :] = _layer_norm(alpha * h_ref[rows, :] + moe, g2_ref[...], b2_ref[...])

    for part in range(COMBINE_PARTS):
        issue(part)
    for part in range(COMBINE_PARTS):
        drain(part)
        finish(part)


def _combine(alpha, slots, h, gate, g2, b2, ys):
    n, d = h.shape
    t = SEQ_TILE
    return pl.pallas_call(
        functools.partial(_combine_kernel, alpha),
        grid=(n // t,),
        in_specs=[
            pl.BlockSpec((1, 1, TOP_K * t), lambda i: (i, 0, 0), memory_space=pltpu.SMEM),
            pl.BlockSpec((t, d), lambda i: (i, 0)),
            pl.BlockSpec((t, LANES), lambda i: (i, 0)),
            pl.BlockSpec((1, d), lambda i: (0, 0)),
            pl.BlockSpec((1, d), lambda i: (0, 0)),
            pl.BlockSpec(memory_space=pl.ANY),
        ],
        out_specs=pl.BlockSpec((t, d), lambda i: (i, 0)),
        out_shape=jax.ShapeDtypeStruct((n, d), _F32),
        scratch_shapes=[pltpu.VMEM((TOP_K, t * SLAB, LANES), _F32), pltpu.SemaphoreType.DMA((COMBINE_PARTS,))],
        compiler_params=_params("arbitrary"),
        name="combine",
    )(slots, h, gate, g2, b2, ys)


def _pad_cols(a, width):
    return jnp.pad(a, ((0, 0), (0, width - a.shape[1])))


def kernel(x, w_in, b_forget, sink, w_proj_swa, w_proj_fox, w_out, ln1_g, ln1_b, w_router, b_router,
           w_gate_up, b_gate_up, w_down, b_down, ln2_g, ln2_b):
    b, s, d = x.shape
    depth = w_in.shape[0]
    n = b * s
    alpha = float((2.0 * depth) ** 0.25)
    assert s % SEQ_TILE == 0 and SEQ_TILE % WINDOW == 0 and d % LANES == 0
    assert w_router.shape[2] == N_EXPERTS and w_in.shape[2] == ATT_W + FOX_HEADS + 2 * d

    blk = EXPERT_ROWS
    n_rows = ((n * TOP_K + blk - 1) // blk) * blk + N_EXPERTS * blk
    n_blocks_pad = ((n_rows // blk + LANES - 1) // LANES) * LANES
    t = SEQ_TILE
    ri = lax.broadcasted_iota(jnp.int32, (t, t), 0)
    ci = lax.broadcasted_iota(jnp.int32, (t, t), 1)
    tri_incl = (ci <= ri).astype(_BF16)
    tri_strict = (ri < ci).astype(_BF16)

    h = x
    for layer in range(depth):
        w_l = w_in[layer]
        w_att = _pad_cols(w_l[:, :ATT_W + FOX_HEADS], ATT_W + LANES).astype(_BF16)
        w_gates = w_l[:, ATT_W + FOX_HEADS:].astype(_BF16)
        bf_pad = _pad_cols(b_forget[layer].reshape(1, FOX_HEADS).astype(_F32), LANES)

        qkv, ccol, kaug = _in_proj(h, w_att, bf_pad, tri_incl)
        att_a = _swa(qkv, sink[layer].reshape(-1).astype(_F32))
        att_f = _fox(qkv, ccol, kaug)

        h1, h1_slabs, idx_t, gate = _mix(
            alpha, h.reshape(n, d), att_a.reshape(n, SWA_Q_W), att_f.reshape(n, FOX_W),
            w_gates, w_proj_swa[layer].astype(_BF16), w_proj_fox[layer].astype(_BF16),
            w_out[layer].astype(_BF16), ln1_g[layer].reshape(1, d), ln1_b[layer].reshape(1, d),
            _pad_cols(w_router[layer], LANES).astype(_BF16),
            _pad_cols(b_router[layer].reshape(1, N_EXPERTS).astype(_F32), LANES))

        slots, meta = _route(idx_t, tri_strict, n_blocks_pad)
        xs = _dispatch(meta, slots, h1_slabs, n_rows)
        ys = _experts(meta, xs, w_gate_up[layer], b_gate_up[layer], w_down[layer], b_down[layer])
        out = _combine(alpha, slots, h1, gate, ln2_g[layer].reshape(1, d), ln2_b[layer].reshape(1, d), ys)
        h = out.reshape(b, s, d)
    return h
```
